```python
import math
import jax, jax.numpy as jnp
from jax import lax
import numpy as np

D_MODEL = 2048
BATCH = 8
SEQ = 2048
DEPTH = 2

N_MIXERS = 2
A_HEAD_DIM = 128
A_HEADS = D_MODEL // (2 * A_HEAD_DIM)
A_V_DIM = 2 * A_HEAD_DIM
A_QK = A_HEADS * 2 * A_HEAD_DIM
A_WIDTH = A_HEADS * A_V_DIM
A_IN = 2 * A_QK + 2 * A_WIDTH
B_GROUPS = ((128, 1), (512, 4), (2048, 16))
B_HEAD_DIM = 128
B_HEADS = D_MODEL // B_HEAD_DIM
B_WIDTH = B_HEADS * B_HEAD_DIM
B_IN = 3 * len(B_GROUPS) * B_WIDTH + B_WIDTH
Q_BLOCK = 128
RMS_EPS = 1e-6

kernel_name = "hybrid_diffattn_dilated_gated"


def rms_norm(x, g):
    xf = x.astype(jnp.float32)
    y = xf * lax.rsqrt(jnp.mean(xf * xf, axis=-1, keepdims=True) + RMS_EPS)
    return (y * g.astype(jnp.float32)).astype(x.dtype)


def alibi_slopes(n):
    return jnp.exp2(-8.0 * (jnp.arange(n, dtype=jnp.float32) + 1.0) / n)


def lambda_init_fn(layer_idx):
    return 0.8 - 0.6 * math.exp(-0.3 * layer_idx)


def diff_attention(q1, q2, k1, k2, v, lam):
    b, s, h, d = q1.shape
    nblk = s // Q_BLOCK
    slopes = alibi_slopes(h)
    kpos = jnp.arange(s)
    scale = d ** -0.5

    def one_block(args):
        qa, qb, i = args
        qpos = i * Q_BLOCK + jnp.arange(Q_BLOCK)
        dist = qpos[:, None] - kpos[None, :]
        bias = -slopes[:, None, None] * dist.astype(jnp.float32)
        causal = dist >= 0
        s1 = jnp.einsum('bqhd,bshd->bhqs', qa, k1).astype(jnp.float32) * scale + bias
        s2 = jnp.einsum('bqhd,bshd->bhqs', qb, k2).astype(jnp.float32) * scale + bias
        p1 = jax.nn.softmax(jnp.where(causal, s1, -jnp.inf), axis=-1)
        p2 = jax.nn.softmax(jnp.where(causal, s2, -jnp.inf), axis=-1)
        p = (p1 - lam * p2).astype(v.dtype)
        return jnp.einsum('bhqs,bshe->bqhe', p, v)

    def to_blocks(t):
        return t.reshape(b, nblk, Q_BLOCK, h, d).transpose(1, 0, 2, 3, 4)

    out = lax.map(one_block, (to_blocks(q1), to_blocks(q2), jnp.arange(nblk)))
    return out.transpose(1, 0, 2, 3, 4).reshape(b, s, h, 2 * d)


def dilated_group(q, k, v, window, dilation, slopes):
    b, s, h, dh = q.shape
    w = window // dilation
    L = s // dilation
    nblk = -(-L // w)
    Lp = nblk * w
    scale = dh ** -0.5

    def split(t):
        t = t.reshape(b, L, dilation, h, dh).transpose(0, 2, 1, 3, 4)
        return jnp.pad(t, ((0, 0), (0, 0), (0, Lp - L), (0, 0), (0, 0)))

    def band(t):
        tb = t.reshape(b, dilation, nblk, w, h, dh)
        prev = jnp.pad(tb, ((0, 0), (0, 0), (1, 0), (0, 0), (0, 0), (0, 0)))[:, :, :-1]
        return jnp.concatenate([prev, tb], axis=3)

    qb = split(q).reshape(b, dilation, nblk, w, h, dh)
    kb = band(split(k))
    vb = band(split(v))
    rel = (jnp.arange(w)[:, None] + w) - jnp.arange(2 * w)[None, :]
    key_sub = jnp.arange(nblk)[:, None] * w - w + jnp.arange(2 * w)[None, :]
    valid = ((rel >= 0) & (rel <= w))[None] & (key_sub >= 0)[:, None, :]
    bias = -slopes[:, None, None] * (rel * dilation).astype(jnp.float32)
    sc = jnp.einsum('brnqhd,brnkhd->brnhqk', qb, kb).astype(jnp.float32) * scale + bias
    sc = jnp.where(valid[:, None], sc, -jnp.inf)
    m = jnp.max(sc, axis=-1, keepdims=True)
    p = jnp.exp(sc - m)
    den = jnp.sum(p, axis=-1, keepdims=True)
    out = jnp.einsum('brnhqk,brnkhd->brnqhd', (p / den).astype(v.dtype), vb)
    lse = (m + jnp.log(den))[..., 0]
    out = out.reshape(b, dilation, Lp, h, dh)[:, :, :L].transpose(0, 2, 1, 3, 4).reshape(b, s, h, dh)
    lse = lse.transpose(0, 1, 2, 4, 3).reshape(b, dilation, Lp, h)[:, :, :L]
    lse = lse.transpose(0, 2, 1, 3).reshape(b, s, h)
    return out, lse


def mixer_a(hn, w_in, lam_q1, lam_k1, lam_q2, lam_k2, subln, w_out, lam_init):
    b, s, _ = hn.shape
    proj = hn @ w_in
    q, k, v, z = jnp.split(proj, [A_QK, 2 * A_QK, 2 * A_QK + A_WIDTH], axis=-1)
    q = q.reshape(b, s, A_HEADS, 2, A_HEAD_DIM)
    k = k.reshape(b, s, A_HEADS, 2, A_HEAD_DIM)
    v = v.reshape(b, s, A_HEADS, A_V_DIM)
    f32 = jnp.float32
    lam = (jnp.exp(jnp.sum(lam_q1.astype(f32) * lam_k1.astype(f32)))
           - jnp.exp(jnp.sum(lam_q2.astype(f32) * lam_k2.astype(f32))) + lam_init)
    o = diff_attention(q[:, :, :, 0], q[:, :, :, 1], k[:, :, :, 0], k[:, :, :, 1], v, lam)
    o = rms_norm(o, subln) * (1.0 - lam_init)
    o = o.reshape(b, s, A_WIDTH) * jax.nn.silu(z)
    return o @ w_out


def mixer_b(hn, w_in, w_out):
    b, s, _ = hn.shape
    n_g = len(B_GROUPS)
    proj = hn @ w_in
    qkv = proj[..., :3 * n_g * B_WIDTH].reshape(b, s, n_g, 3, B_HEADS, B_HEAD_DIM)
    z = proj[..., 3 * n_g * B_WIDTH:]
    slopes = alibi_slopes(B_HEADS)
    outs, lses = [], []
    for g, (window, dil) in enumerate(B_GROUPS):
        o, l = dilated_group(qkv[:, :, g, 0], qkv[:, :, g, 1], qkv[:, :, g, 2], window, dil, slopes)
        outs.append(o)
        lses.append(l)
    alpha = jax.nn.softmax(jnp.stack(lses, axis=0), axis=0)
    o = jnp.sum(alpha[..., None].astype(hn.dtype) * jnp.stack(outs, axis=0), axis=0)
    o = o.reshape(b, s, B_WIDTH) * jax.nn.silu(z)
    return o @ w_out


def setup_inputs(seed: int = 0) -> dict:
    key = jax.random.key(seed)
    ks = jax.random.split(key, 13)
    n_a = (DEPTH + 1) // 2
    n_b = DEPTH // 2
    nrm = jax.random.normal
    f32 = jnp.float32
    return {
        "x": nrm(ks[0], (BATCH, SEQ, D_MODEL), f32),
        "norm_a": 1.0 + 0.1 * nrm(ks[1], (n_a, D_MODEL), f32),
        "w_in_a": nrm(ks[2], (n_a, D_MODEL, A_IN), f32) * D_MODEL ** -0.5,
        "lam_q1": 0.1 * nrm(ks[3], (n_a, A_HEAD_DIM), f32),
        "lam_k1": 0.1 * nrm(ks[4], (n_a, A_HEAD_DIM), f32),
        "lam_q2": 0.1 * nrm(ks[5], (n_a, A_HEAD_DIM), f32),
        "lam_k2": 0.1 * nrm(ks[6], (n_a, A_HEAD_DIM), f32),
        "subln_a": 1.0 + 0.1 * nrm(ks[7], (n_a, A_V_DIM), f32),
        "w_out_a": nrm(ks[8], (n_a, A_WIDTH, D_MODEL), f32) * A_WIDTH ** -0.5,
        "norm_b": 1.0 + 0.1 * nrm(ks[9], (n_b, D_MODEL), f32),
        "w_in_b": nrm(ks[10], (n_b, D_MODEL, B_IN), f32) * D_MODEL ** -0.5,
        "w_out_b": nrm(ks[11], (n_b, B_WIDTH, D_MODEL), f32) * B_WIDTH ** -0.5,
        "norm_f": 1.0 + 0.1 * nrm(ks[12], (D_MODEL,), f32),
    }


def reference(x, norm_a, w_in_a, lam_q1, lam_k1, lam_q2, lam_k2, subln_a, w_out_a,
              norm_b, w_in_b, w_out_b, norm_f):
    h = x
    for i in range(DEPTH):
        j = i // N_MIXERS
        if i % N_MIXERS == 0:
            y = mixer_a(rms_norm(h, norm_a[j]), w_in_a[j], lam_q1[j], lam_k1[j], lam_q2[j],
                        lam_k2[j], subln_a[j], w_out_a[j], lambda_init_fn(i))
        else:
            y = mixer_b(rms_norm(h, norm_b[j]), w_in_b[j], w_out_b[j])
        h = h + y
    return rms_norm(h, norm_f)
```

```python
import functools
import math

import jax
import jax.numpy as jnp
from jax import lax
from jax.experimental import pallas as pl
from jax.experimental.pallas import tpu as pltpu

D_MODEL = 2048
A_HEAD_DIM = 128
A_HEADS = D_MODEL // (2 * A_HEAD_DIM)
A_V_DIM = 2 * A_HEAD_DIM
B_GROUPS = ((128, 1), (512, 4), (2048, 16))
B_HEAD_DIM = 128
B_HEADS = D_MODEL // B_HEAD_DIM
B_WIDTH = B_HEADS * B_HEAD_DIM
RMS_EPS = 1e-6
NEG_BIG = -1e30

VMEM_LIMIT_BYTES = 48 * 1024 * 1024


def _params(*semantics):
    return pltpu.CompilerParams(dimension_semantics=semantics, vmem_limit_bytes=VMEM_LIMIT_BYTES)


def _norm_kernel(x_ref, g_ref, o_ref):
    x = x_ref[0]
    ms = jnp.mean(x * x, axis=-1, keepdims=True)
    o_ref[0, 0] = (x * lax.rsqrt(ms + RMS_EPS) * g_ref[...]).astype(o_ref.dtype)


def _rmsnorm_residue_major(x, gain, dil, tu):
    b, s, d = x.shape
    L = s // dil
    xv = x.reshape(b, L, dil * d)
    out = pl.pallas_call(
        _norm_kernel,
        grid=(b, dil, L // tu),
        in_specs=[pl.BlockSpec((1, tu, d), lambda i, r, u: (i, u, r)),
                  pl.BlockSpec((1, d), lambda i, r, u: (0, 0))],
        out_specs=pl.BlockSpec((1, 1, tu, d), lambda i, r, u: (i, r, u, 0)),
        out_shape=jax.ShapeDtypeStruct((b, dil, L, d), jnp.bfloat16),
        compiler_params=_params("parallel", "parallel", "parallel"),
        name=f"rmsnorm_dil{dil}",
    )(xv, gain.reshape(1, d))
    return out.reshape(b * s, d)


def _proj_kernel(a_ref, w_ref, o_ref, *, cb):
    acc = jnp.dot(a_ref[...], w_ref[...], preferred_element_type=jnp.float32)
    for c in range(o_ref.shape[0]):
        o_ref[c] = acc[:, c * cb:(c + 1) * cb].astype(o_ref.dtype)


def _project(a, w, n_out, col_block_map, cb, name, tm=1024, tn=1024):
    m, k = a.shape
    return pl.pallas_call(
        functools.partial(_proj_kernel, cb=cb),
        grid=(m // tm, n_out // tn),
        in_specs=[pl.BlockSpec((tm, k), lambda i, j: (i, 0)),
                  pl.BlockSpec((k, tn), lambda i, j: (0, col_block_map(j)))],
        out_specs=pl.BlockSpec((tn // cb, tm, cb), lambda i, j: (j, i, 0)),
        out_shape=jax.ShapeDtypeStruct((n_out // cb, m, cb), jnp.bfloat16),
        compiler_params=_params("parallel", "arbitrary"),
        name=name,
    )(a, w)


def _out_proj_kernel(a_ref, w_ref, r_ref, g_ref, o_ref, *, final_norm):
    a = jnp.concatenate([a_ref[c] for c in range(a_ref.shape[0])], axis=-1)
    y = jnp.dot(a, w_ref[...], preferred_element_type=jnp.float32) + r_ref[...]
    if final_norm:
        ms = jnp.mean(y * y, axis=-1, keepdims=True)
        y = y * lax.rsqrt(ms + RMS_EPS) * g_ref[...]
    o_ref[...] = y


def _out_project(a, w, resid, gain, final_norm, name, tm=512):
    nc, m, cb = a.shape
    k, n = w.shape
    return pl.pallas_call(
        functools.partial(_out_proj_kernel, final_norm=final_norm),
        grid=(m // tm,),
        in_specs=[pl.BlockSpec((nc, tm, cb), lambda i: (0, i, 0)),
                  pl.BlockSpec((k, n), lambda i: (0, 0)),
                  pl.BlockSpec((tm, n), lambda i: (i, 0)),
                  pl.BlockSpec((1, n), lambda i: (0, 0))],
        out_specs=pl.BlockSpec((tm, n), lambda i: (i, 0)),
        out_shape=jax.ShapeDtypeStruct((m, n), jnp.float32),
        compiler_params=_params("parallel"),
        name=name,
    )(a, w, resid, gain.reshape(1, n))


def _attn_a_kernel(q_ref, k_ref, v_ref, z_ref, lq1_ref, lk1_ref, lq2_ref, lk2_ref, sub_ref,
                   o_ref, m_sc, l_sc, acc_sc, *, tq, n_heads, lam_init):
    hd = A_HEAD_DIM
    h = pl.program_id(1)
    qi = pl.program_id(2)
    scale = hd ** -0.5
    slope = jnp.exp2(jnp.zeros((1, 1), jnp.float32)
                     - (8.0 / n_heads) * (h + 1).astype(jnp.float32))
    rel0 = (lax.broadcasted_iota(jnp.int32, (tq, tq), 0)
            - lax.broadcasted_iota(jnp.int32, (tq, tq), 1)).astype(jnp.float32)

    q = q_ref[0]
    qs = (q[:, :hd], q[:, hd:])

    m_sc[...] = jnp.full(m_sc.shape, -jnp.inf, jnp.float32)
    l_sc[...] = jnp.zeros(l_sc.shape, jnp.float32)
    acc_sc[...] = jnp.zeros(acc_sc.shape, jnp.float32)

    def chunk(c, masked):
        row0 = pl.multiple_of(c * tq, tq)
        kc = k_ref[0, pl.ds(row0, tq), :]
        vc = v_ref[0, pl.ds(row0, tq), :]
        dist = rel0 + ((qi - c) * tq).astype(jnp.float32)
        bias = -slope * dist
        for i in range(2):
            ki = kc[:, i * hd:(i + 1) * hd]
            s = lax.dot_general(qs[i], ki, (((1,), (1,)), ((), ())),
                                preferred_element_type=jnp.float32) * scale + bias
            if masked:
                s = jnp.where(dist >= 0, s, -jnp.inf)
            m_prev = m_sc[i]
            m_new = jnp.maximum(m_prev, jnp.max(s, axis=-1, keepdims=True))
            alpha = jnp.exp(m_prev - m_new)
            p = jnp.exp(s - m_new)
            l_sc[i] = alpha * l_sc[i] + jnp.sum(p, axis=-1, keepdims=True)
            acc_sc[i] = alpha * acc_sc[i] + jnp.dot(p.astype(vc.dtype), vc,
                                                    preferred_element_type=jnp.float32)
            m_sc[i] = m_new

    def body(c, carry):
        chunk(c, False)
        return carry

    lax.fori_loop(0, qi, body, 0)
    chunk(qi, True)

    lam = (jnp.exp(jnp.sum(lq1_ref[...] * lk1_ref[...], axis=-1, keepdims=True))
           - jnp.exp(jnp.sum(lq2_ref[...] * lk2_ref[...], axis=-1, keepdims=True)) + lam_init)
    o = acc_sc[0] / l_sc[0] - lam * (acc_sc[1] / l_sc[1])
    ms = jnp.mean(o * o, axis=-1, keepdims=True)
    o = o * lax.rsqrt(ms + RMS_EPS) * sub_ref[...] * (1.0 - lam_init)
    z = z_ref[0].astype(jnp.float32)
    o_ref[0] = (o * (z * jax.nn.sigmoid(z))).astype(o_ref.dtype)


def _attention_a(proj, lam_q1, lam_k1, lam_q2, lam_k2, subln, batch, seq, lam_init, tq=256):
    nh = A_HEADS
    nq = seq // tq
    w2 = 2 * A_HEAD_DIM
    vec = lambda a: a.reshape(1, -1)
    vec_spec = pl.BlockSpec((1, A_HEAD_DIM), lambda b, h, i: (0, 0))
    return pl.pallas_call(
        functools.partial(_attn_a_kernel, tq=tq, n_heads=nh, lam_init=lam_init),
        grid=(batch, nh, nq),
        in_specs=[pl.BlockSpec((1, tq, w2), lambda b, h, i: (h, b * nq + i, 0)),
                  pl.BlockSpec((1, seq, w2), lambda b, h, i: (nh + h, b, 0)),
                  pl.BlockSpec((1, seq, w2), lambda b, h, i: (2 * nh + h, b, 0)),
                  pl.BlockSpec((1, tq, w2), lambda b, h, i: (3 * nh + h, b * nq + i, 0)),
                  vec_spec, vec_spec, vec_spec, vec_spec,
                  pl.BlockSpec((1, w2), lambda b, h, i: (0, 0))],
        out_specs=pl.BlockSpec((1, tq, w2), lambda b, h, i: (h, b * nq + i, 0)),
        out_shape=jax.ShapeDtypeStruct((nh, batch * seq, w2), jnp.bfloat16),
        scratch_shapes=[pltpu.VMEM((2, tq, 1), jnp.float32),
                        pltpu.VMEM((2, tq, 1), jnp.float32),
                        pltpu.VMEM((2, tq, w2), jnp.float32)],
        compiler_params=_params("parallel", "parallel", "arbitrary"),
        name="diff_attention",
    )(proj, proj, proj, proj, vec(lam_q1), vec(lam_k1), vec(lam_q2), vec(lam_k2), vec(subln))


def _attn_b_kernel(q0_ref, k0_ref, v0_ref, q1_ref, k1_ref, v1_ref, q2_ref, k2_ref, v2_ref, z_ref,
                   o_ref, out_sc, lse_sc, *, seq, n_heads, groups):
    hd = B_HEAD_DIM
    h = pl.program_id(1)
    scale = hd ** -0.5
    slope = jnp.exp2(jnp.zeros((1, 1), jnp.float32)
                     - (8.0 / n_heads) * (h + 1).astype(jnp.float32))
    qkv = ((q0_ref, k0_ref, v0_ref), (q1_ref, k1_ref, v1_ref), (q2_ref, k2_ref, v2_ref))

    for g, (window, dil) in enumerate(groups):
        w = window // dil
        n_per = (seq // dil) // w
        n_keys = 2 * w if n_per > 1 else w
        q_ref, k_ref, v_ref = qkv[g]
        ii = lax.broadcasted_iota(jnp.int32, (w, n_keys), 0)
        jj = lax.broadcasted_iota(jnp.int32, (w, n_keys), 1)
        rel = ii + (n_keys - w) - jj
        band = (rel >= 0) & (rel <= w)
        bias = jnp.where(band, -slope * (rel * dil).astype(jnp.float32), NEG_BIG)

        def block(n, carry, w=w, n_per=n_per, n_keys=n_keys, dil=dil, g=g,
                  q_ref=q_ref, k_ref=k_ref, v_ref=v_ref, bias=bias, jj=jj):
            res = n // n_per
            nb = n % n_per
            row0 = pl.multiple_of(n * w, w)
            q = q_ref[0, pl.ds(row0, w), :]
            kc = k_ref[0, pl.ds(row0, w), :]
            vc = v_ref[0, pl.ds(row0, w), :]
            if n_keys > w:
                prev0 = pl.multiple_of(jnp.maximum(n - 1, 0) * w, w)
                kc = jnp.concatenate([k_ref[0, pl.ds(prev0, w), :], kc], axis=0)
                vc = jnp.concatenate([v_ref[0, pl.ds(prev0, w), :], vc], axis=0)
            s = lax.dot_general(q, kc, (((1,), (1,)), ((), ())),
                                preferred_element_type=jnp.float32) * scale + bias
            if n_keys > w:
                s = jnp.where(jj + jnp.where(nb > 0, w, 0) >= w, s, NEG_BIG)
            m = jnp.max(s, axis=-1, keepdims=True)
            p = jnp.exp(s - m)
            den = jnp.sum(p, axis=-1, keepdims=True)
            o = jnp.dot(p.astype(vc.dtype), vc, preferred_element_type=jnp.float32) / den
            lse = m + jnp.log(den)
            t0 = nb * (w * dil) + res
            rows = pl.ds(t0, w, stride=dil) if dil > 1 else pl.ds(t0, w)
            out_sc[g, rows, :] = o
            lse_sc[g, rows, :] = jnp.broadcast_to(lse, (w, hd))
            return carry

        lax.fori_loop(0, seq // w, block, 0)

    tc = 256

    def combine(c, carry):
        rows = pl.ds(pl.multiple_of(c * tc, tc), tc)
        lses = [lse_sc[g, rows, :] for g in range(len(groups))]
        top = functools.reduce(jnp.maximum, lses)
        wts = [jnp.exp(l - top) for l in lses]
        num = functools.reduce(lambda a, b: a + b,
                               [wt * out_sc[g, rows, :] for g, wt in enumerate(wts)])
        o = num / functools.reduce(lambda a, b: a + b, wts)
        z = z_ref[0, rows, :].astype(jnp.float32)
        o_ref[0, rows, :] = (o * (z * jax.nn.sigmoid(z))).astype(o_ref.dtype)
        return carry

    lax.fori_loop(0, seq // tc, combine, 0)


def _attention_b(proj_nat, proj_g1, proj_g2, batch, seq):
    nh = B_HEADS
    hd = B_HEAD_DIM
    slab = lambda off: pl.BlockSpec((1, seq, hd), lambda b, h: (off + h, b, 0))
    return pl.pallas_call(
        functools.partial(_attn_b_kernel, seq=seq, n_heads=nh, groups=B_GROUPS),
        grid=(batch, nh),
        in_specs=[slab(0), slab(nh), slab(2 * nh),
                  slab(0), slab(nh), slab(2 * nh),
                  slab(0), slab(nh), slab(2 * nh),
                  slab(3 * nh)],
        out_specs=pl.BlockSpec((1, seq, hd), lambda b, h: (h, b, 0)),
        out_shape=jax.ShapeDtypeStruct((nh, batch * seq, hd), jnp.bfloat16),
        scratch_shapes=[pltpu.VMEM((len(B_GROUPS), seq, hd), jnp.float32),
                        pltpu.VMEM((len(B_GROUPS), seq, hd), jnp.float32)],
        compiler_params=_params("parallel", "parallel"),
        name="dilated_attention",
    )(proj_nat, proj_nat, proj_nat, proj_g1, proj_g1, proj_g1, proj_g2, proj_g2, proj_g2, proj_nat)


def kernel(x, norm_a, w_in_a, lam_q1, lam_k1, lam_q2, lam_k2, subln_a, w_out_a,
           norm_b, w_in_b, w_out_b, norm_f):
    batch, seq, d = x.shape
    m = batch * seq
    bf16 = jnp.bfloat16
    x2 = x.reshape(m, d)

    lam_init = 0.8 - 0.6 * math.exp(-0.3 * 0)
    hn = _rmsnorm_residue_major(x, norm_a[0], 1, 512)
    proj_a = _project(hn, w_in_a[0].astype(bf16), w_in_a.shape[-1], lambda j: j, 2 * A_HEAD_DIM,
                      "in_proj_a")
    gated_a = _attention_a(proj_a, lam_q1[0], lam_k1[0], lam_q2[0], lam_k2[0], subln_a[0],
                           batch, seq, lam_init)
    h1 = _out_project(gated_a, w_out_a[0].astype(bf16), x2, norm_f, False, "out_proj_a")

    h1_3d = h1.reshape(batch, seq, d)
    w_in = w_in_b[0].astype(bf16)
    tn = 1024
    group_cols = 3 * B_WIDTH
    gate_blk0 = len(B_GROUPS) * group_cols // tn
    projs = []
    for g, (_, dil) in enumerate(B_GROUPS):
        hn_g = _rmsnorm_residue_major(h1_3d, norm_b[0], dil, min(512, seq // dil))
        blk0 = g * group_cols // tn
        if g == 0:
            n_grp = group_cols // tn
            col_map = lambda j: jnp.where(j < n_grp, j, j + (gate_blk0 - n_grp))
            projs.append(_project(hn_g, w_in, group_cols + B_WIDTH, col_map, B_HEAD_DIM,
                                  "in_proj_b0", tn=tn))
        else:
            projs.append(_project(hn_g, w_in, group_cols, lambda j, blk0=blk0: blk0 + j,
                                  B_HEAD_DIM, f"in_proj_b{g}", tn=tn))
    gated_b = _attention_b(projs[0], projs[1], projs[2], batch, seq)
    out = _out_project(gated_b, w_out_b[0].astype(bf16), h1, norm_f, True, "out_proj_b")
    return out.reshape(batch, seq, d)
```

```python
import functools
import math

import jax
import jax.numpy as jnp
from jax import lax
from jax.experimental import pallas as pl
from jax.experimental.pallas import tpu as pltpu

A_HEAD_DIM = 128
B_GROUPS = ((128, 1), (512, 4), (2048, 16))
B_HEAD_DIM = 128
RMS_EPS = 1e-6
NEG_BIG = -1e30
LOG2E = math.log2(math.e)
LANES = 128

VMEM_LIMIT_BYTES = 48 * 1024 * 1024


def _params(*semantics):
    return pltpu.CompilerParams(dimension_semantics=semantics, vmem_limit_bytes=VMEM_LIMIT_BYTES)


def _alibi_slope_log2(h, n_heads):
    return LOG2E * jnp.exp2(jnp.zeros((1, 1), jnp.float32)
                            - (8.0 / n_heads) * (h + 1).astype(jnp.float32))


def _silu(z):
    return z * jax.nn.sigmoid(z)


def _aligned(start, multiple):
    return start if isinstance(start, int) else pl.multiple_of(start, multiple)


def _norm_kernel(x_ref, g_ref, o_ref):
    x = x_ref[...]
    ms = jnp.mean(x * x, axis=-1, keepdims=True)
    o_ref[...] = (x * lax.rsqrt(ms + RMS_EPS) * g_ref[...]).astype(o_ref.dtype)


def _rmsnorm(x, gain, tm=512):
    m, d = x.shape
    return pl.pallas_call(
        _norm_kernel,
        grid=(m // tm,),
        in_specs=[pl.BlockSpec((tm, d), lambda i: (i, 0)),
                  pl.BlockSpec((1, d), lambda i: (0, 0))],
        out_specs=pl.BlockSpec((tm, d), lambda i: (i, 0)),
        out_shape=jax.ShapeDtypeStruct((m, d), jnp.bfloat16),
        compiler_params=_params("parallel"),
        name="rmsnorm",
    )(x, gain.reshape(1, d))


def _proj_kernel(a_ref, w_ref, o_ref, *, cb, n_scaled, col_scale):
    acc = jnp.dot(a_ref[...], w_ref[...], preferred_element_type=jnp.float32)
    if n_scaled:
        acc = acc * jnp.where(pl.program_id(1) < n_scaled, col_scale, 1.0)
    for c in range(o_ref.shape[0]):
        o_ref[c] = acc[:, c * cb:(c + 1) * cb].astype(o_ref.dtype)


def _project(a, w, n_out, col_block_map, cb, name, n_scaled=0, col_scale=1.0, tm=1024, tn=1024):
    m, k = a.shape
    return pl.pallas_call(
        functools.partial(_proj_kernel, cb=cb, n_scaled=n_scaled, col_scale=col_scale),
        grid=(m // tm, n_out // tn),
        in_specs=[pl.BlockSpec((tm, k), lambda i, j: (i, 0)),
                  pl.BlockSpec((k, tn), lambda i, j: (0, col_block_map(j)))],
        out_specs=pl.BlockSpec((tn // cb, tm, cb), lambda i, j: (j, i, 0)),
        out_shape=jax.ShapeDtypeStruct((n_out // cb, m, cb), jnp.bfloat16),
        compiler_params=_params("parallel", "arbitrary"),
        name=name,
    )(a, w)


def _proj_t_kernel(wt_ref, a_ref, o_ref, *, cb):
    acc = lax.dot_general(wt_ref[...], a_ref[...], (((1,), (1,)), ((), ())),
                          preferred_element_type=jnp.float32)
    for c in range(o_ref.shape[0]):
        o_ref[c] = acc[:, c * cb:(c + 1) * cb].astype(o_ref.dtype)


def _project_transposed(a, wt, cb, name, tm=1024, tn=1024):
    m, k = a.shape
    n = wt.shape[0]
    return pl.pallas_call(
        functools.partial(_proj_t_kernel, cb=cb),
        grid=(m // tm, n // tn),
        in_specs=[pl.BlockSpec((tn, k), lambda i, j: (j, 0)),
                  pl.BlockSpec((tm, k), lambda i, j: (i, 0))],
        out_specs=pl.BlockSpec((tm // cb, tn, cb), lambda i, j: (i, j, 0)),
        out_shape=jax.ShapeDtypeStruct((m // cb, n, cb), jnp.bfloat16),
        compiler_params=_params("parallel", "arbitrary"),
        name=name,
    )(wt, a)


def _gather_slabs(a_ref):
    return jnp.concatenate([a_ref[c] for c in range(a_ref.shape[0])], axis=-1)


def _out_proj_a_kernel(a_ref, w_ref, r_ref, g_ref, h_ref, hn1_ref, hn4_ref, hn16_ref, slab_sc):
    y = jnp.dot(_gather_slabs(a_ref), w_ref[...], preferred_element_type=jnp.float32) + r_ref[...]
    h_ref[...] = y
    ms = jnp.mean(y * y, axis=-1, keepdims=True)
    hn = y * lax.rsqrt(ms + RMS_EPS) * g_ref[...]
    hn1_ref[...] = hn.astype(hn1_ref.dtype)
    tm, d = hn.shape
    n_slab = d // LANES
    for c in range(n_slab):
        slab_sc[c] = hn[:, c * LANES:(c + 1) * LANES]
    for ref in (hn4_ref, hn16_ref):
        dil = ref.shape[1]
        for r in range(dil):
            rows = jnp.concatenate(
                [slab_sc[c, pl.ds(r, tm // dil, stride=dil), :] for c in range(n_slab)], axis=-1)
            ref[0, r] = rows.astype(ref.dtype)


def _out_project_a(a, w, resid, gain, batch, seq, tm=256):
    nc, m, cb = a.shape
    k, n = w.shape
    nb = seq // tm
    const = lambda shape: pl.BlockSpec(shape, lambda i: (0,) * len(shape),
                                       pipeline_mode=pl.Buffered(1))
    perm_spec = lambda dil: pl.BlockSpec((1, dil, tm // dil, n), lambda i: (i // nb, 0, i % nb, 0))
    perm_shape = lambda dil: jax.ShapeDtypeStruct((batch, dil, seq // dil, n), jnp.bfloat16)
    return pl.pallas_call(
        _out_proj_a_kernel,
        grid=(m // tm,),
        in_specs=[pl.BlockSpec((nc, tm, cb), lambda i: (0, i, 0)),
                  const((k, n)),
                  pl.BlockSpec((tm, n), lambda i: (i, 0)),
                  const((1, n))],
        out_specs=[pl.BlockSpec((tm, n), lambda i: (i, 0)),
                   pl.BlockSpec((tm, n), lambda i: (i, 0)),
                   perm_spec(4), perm_spec(16)],
        out_shape=[jax.ShapeDtypeStruct((m, n), jnp.float32),
                   jax.ShapeDtypeStruct((m, n), jnp.bfloat16),
                   perm_shape(4), perm_shape(16)],
        scratch_shapes=[pltpu.VMEM((n // LANES, tm, LANES), jnp.float32)],
        compiler_params=_params("parallel"),
        name="out_proj_a",
    )(a, w, resid, gain.reshape(1, n))


def _out_proj_b_kernel(a_ref, w_ref, r_ref, g_ref, o_ref):
    y = jnp.dot(_gather_slabs(a_ref), w_ref[...], preferred_element_type=jnp.float32) + r_ref[...]
    ms = jnp.mean(y * y, axis=-1, keepdims=True)
    o_ref[...] = y * lax.rsqrt(ms + RMS_EPS) * g_ref[...]


def _out_project_b(a, w, resid, gain, tm=512):
    nc, m, cb = a.shape
    k, n = w.shape
    const = lambda shape: pl.BlockSpec(shape, lambda i: (0,) * len(shape),
                                       pipeline_mode=pl.Buffered(1))
    return pl.pallas_call(
        _out_proj_b_kernel,
        grid=(m // tm,),
        in_specs=[pl.BlockSpec((nc, tm, cb), lambda i: (0, i, 0)),
                  const((k, n)),
                  pl.BlockSpec((tm, n), lambda i: (i, 0)),
                  const((1, n))],
        out_specs=pl.BlockSpec((tm, n), lambda i: (i, 0)),
        out_shape=jax.ShapeDtypeStruct((m, n), jnp.float32),
        compiler_params=_params("parallel"),
        name="out_proj_b",
    )(a, w, resid, gain.reshape(1, n))


def _attn_a_kernel(q_ref, k_ref, vt_ref, z_ref, lq1_ref, lk1_ref, lq2_ref, lk2_ref, sub_ref,
                   o_ref, m_sc, l_sc, acc_sc, *, tq, n_heads, lam_init):
    hd = A_HEAD_DIM
    h = pl.program_id(1)
    qi = pl.program_id(2)
    slope = _alibi_slope_log2(h, n_heads)
    rel0 = (lax.broadcasted_iota(jnp.int32, (tq, tq), 1)
            - lax.broadcasted_iota(jnp.int32, (tq, tq), 0))
    base = -slope * rel0.astype(jnp.float32)

    q = q_ref[0]
    qs = (q[:, :hd], q[:, hd:])

    m_sc[...] = jnp.full(m_sc.shape, -jnp.inf, jnp.float32)
    l_sc[...] = jnp.zeros(l_sc.shape, jnp.float32)
    acc_sc[...] = jnp.zeros(acc_sc.shape, jnp.float32)

    def update(c0, n_tiles, ends_on_diagonal):
        tiles = [c0 + u for u in range(n_tiles)]
        kcs = [k_ref[0, pl.ds(_aligned(c * tq, tq), tq), :] for c in tiles]
        deltas = [-slope * ((qi - c) * tq).astype(jnp.float32) for c in tiles]
        scores = []
        for i in range(2):
            for u, kc in enumerate(kcs):
                s = lax.dot_general(kc[:, i * hd:(i + 1) * hd], qs[i], (((1,), (1,)), ((), ())),
                                    preferred_element_type=jnp.float32) + base
                if ends_on_diagonal and u == n_tiles - 1:
                    s = jnp.where(rel0 >= 0, s, -jnp.inf)
                scores.append(s)
        probs, alphas = [], []
        for i in range(2):
            ss = scores[i * n_tiles:(i + 1) * n_tiles]
            m_prev = m_sc[i]
            m_new = functools.reduce(
                jnp.maximum, [jnp.max(s, axis=0, keepdims=True) + d for s, d in zip(ss, deltas)],
                m_prev)
            alpha = jnp.exp2(m_prev - m_new)
            ps = [jnp.exp2(s - (m_new - d)) for s, d in zip(ss, deltas)]
            l_sc[i] = alpha * l_sc[i] + functools.reduce(
                lambda a, b: a + b, [jnp.sum(p, axis=0, keepdims=True) for p in ps])
            m_sc[i] = m_new
            alphas.append(alpha)
            probs.append([p.astype(vt_ref.dtype) for p in ps])
        for i in range(2):
            pv = functools.reduce(
                lambda a, b: a + b,
                [jnp.dot(vt_ref[c], p, preferred_element_type=jnp.float32)
                 for c, p in zip(tiles, probs[i])])
            acc_sc[i] = alphas[i] * acc_sc[i] + pv

    def pair(it, carry):
        update(2 * it, 2, False)
        return carry

    lax.fori_loop(0, qi // 2, pair, 0)

    @pl.when(qi % 2 == 0)
    def _():
        update(qi, 1, True)

    @pl.when(qi % 2 == 1)
    def _():
        update(qi - 1, 2, True)

    lam = (jnp.exp(jnp.sum(lq1_ref[...] * lk1_ref[...], axis=-1, keepdims=True))
           - jnp.exp(jnp.sum(lq2_ref[...] * lk2_ref[...], axis=-1, keepdims=True)) + lam_init)
    o = (acc_sc[0] / l_sc[0] - lam * (acc_sc[1] / l_sc[1])).T
    ms = jnp.mean(o * o, axis=-1, keepdims=True)
    o = o * lax.rsqrt(ms + RMS_EPS) * sub_ref[...] * (1.0 - lam_init)
    o_ref[0] = (o * _silu(z_ref[0].astype(jnp.float32))).astype(o_ref.dtype)


def _attention_a(proj, vt, lam_q1, lam_k1, lam_q2, lam_k2, subln, batch, seq, lam_init, tq=256):
    w2 = 2 * A_HEAD_DIM
    nh = proj.shape[0] // 3
    nq = seq // tq
    vec = lambda a: a.reshape(1, -1)
    vec_spec = pl.BlockSpec((1, A_HEAD_DIM), lambda b, h, i: (0, 0))
    return pl.pallas_call(
        functools.partial(_attn_a_kernel, tq=tq, n_heads=nh, lam_init=lam_init),
        grid=(batch, nh, nq),
        in_specs=[pl.BlockSpec((1, tq, w2), lambda b, h, i: (h, b * nq + i, 0)),
                  pl.BlockSpec((1, seq, w2), lambda b, h, i: (nh + h, b, 0)),
                  pl.BlockSpec((nq, w2, tq), lambda b, h, i: (b, h, 0)),
                  pl.BlockSpec((1, tq, w2), lambda b, h, i: (2 * nh + h, b * nq + i, 0)),
                  vec_spec, vec_spec, vec_spec, vec_spec,
                  pl.BlockSpec((1, w2), lambda b, h, i: (0, 0))],
        out_specs=pl.BlockSpec((1, tq, w2), lambda b, h, i: (h, b * nq + i, 0)),
        out_shape=jax.ShapeDtypeStruct((nh, batch * seq, w2), jnp.bfloat16),
        scratch_shapes=[pltpu.VMEM((2, 1, tq), jnp.float32),
                        pltpu.VMEM((2, 1, tq), jnp.float32),
                        pltpu.VMEM((2, w2, tq), jnp.float32)],
        compiler_params=_params("parallel", "parallel", "arbitrary"),
        name="diff_attention",
    )(proj, proj, vt, proj, vec(lam_q1), vec(lam_k1), vec(lam_q2), vec(lam_k2), vec(subln))


_B_BATCH = 8


def _attn_b_kernel(q0_ref, k0_ref, v0_ref, q1_ref, k1_ref, v1_ref, q2_ref, k2_ref, v2_ref, z_ref,
                   o_ref, out_sc, lse_sc, *, seq, n_heads, groups):
    hd = B_HEAD_DIM
    h = pl.program_id(1)
    slope = _alibi_slope_log2(h, n_heads)
    qkv = ((q0_ref, k0_ref, v0_ref), (q1_ref, k1_ref, v1_ref), (q2_ref, k2_ref, v2_ref))

    for g, (window, dil) in enumerate(groups):
        w = window // dil
        n_per = (seq // dil) // w
        q_ref, k_ref, v_ref = qkv[g]

        def band_bias(n_keys, w=w, dil=dil):
            ii = lax.broadcasted_iota(jnp.int32, (w, n_keys), 0)
            jj = lax.broadcasted_iota(jnp.int32, (w, n_keys), 1)
            rel = ii + (n_keys - w) - jj
            return jnp.where((rel >= 0) & (rel <= w),
                             -slope * (rel * dil).astype(jnp.float32), NEG_BIG)

        biases = {w: band_bias(w)}
        if n_per > 1:
            biases[2 * w] = band_bias(2 * w)
        n_blocks = seq // w
        for b0 in range(0, n_blocks, _B_BATCH):
            blocks = []
            for n in range(b0, min(b0 + _B_BATCH, n_blocks)):
                res, nb = divmod(n, n_per)
                n_keys = w if nb == 0 else 2 * w
                key_rows = pl.ds((n + 1) * w - n_keys, n_keys)
                q = q_ref[0, pl.ds(n * w, w), :]
                s = lax.dot_general(q, k_ref[0, key_rows, :], (((1,), (1,)), ((), ())),
                                    preferred_element_type=jnp.float32) + biases[n_keys]
                blocks.append((res, nb, key_rows, s))
            soft = []
            for res, nb, key_rows, s in blocks:
                m = jnp.max(s, axis=-1, keepdims=True)
                p = jnp.exp2(s - m)
                den = jnp.sum(p, axis=-1, keepdims=True)
                soft.append((p.astype(v_ref.dtype), den, m + jnp.log2(den)))
            for (res, nb, key_rows, _), (p, den, lse) in zip(blocks, soft):
                o = jnp.dot(p, v_ref[0, key_rows, :], preferred_element_type=jnp.float32) / den
                t0 = nb * (w * dil) + res
                dst = pl.ds(t0, w, stride=dil) if dil > 1 else pl.ds(t0, w)
                out_sc[g, dst, :] = o
                lse_sc[g, dst, :] = jnp.broadcast_to(lse, (w, hd))

    tc = 256

    def combine(c, carry):
        rows = pl.ds(pl.multiple_of(c * tc, tc), tc)
        lses = [lse_sc[g, rows, :] for g in range(len(groups))]
        top = functools.reduce(jnp.maximum, lses)
        wts = [jnp.exp2(l - top) for l in lses]
        num = functools.reduce(lambda a, b: a + b,
                               [wt * out_sc[g, rows, :] for g, wt in enumerate(wts)])
        o = num / functools.reduce(lambda a, b: a + b, wts)
        o_ref[0, rows, :] = (o * _silu(z_ref[0, rows, :].astype(jnp.float32))).astype(o_ref.dtype)
        return carry

    lax.fori_loop(0, seq // tc, combine, 0)


def _attention_b(proj_nat, proj_g1, proj_g2, batch, seq):
    hd = B_HEAD_DIM
    nh = proj_g1.shape[0] // 3
    assert all(seq % window == 0 for window, _ in B_GROUPS)
    slab = lambda off: pl.BlockSpec((1, seq, hd), lambda b, h: (off + h, b, 0))
    return pl.pallas_call(
        functools.partial(_attn_b_kernel, seq=seq, n_heads=nh, groups=B_GROUPS),
        grid=(batch, nh),
        in_specs=[slab(0), slab(nh), slab(2 * nh),
                  slab(0), slab(nh), slab(2 * nh),
                  slab(0), slab(nh), slab(2 * nh),
                  slab(3 * nh)],
        out_specs=pl.BlockSpec((1, seq, hd), lambda b, h: (h, b, 0)),
        out_shape=jax.ShapeDtypeStruct((nh, batch * seq, hd), jnp.bfloat16),
        scratch_shapes=[pltpu.VMEM((len(B_GROUPS), seq, hd), jnp.float32),
                        pltpu.VMEM((len(B_GROUPS), seq, hd), jnp.float32)],
        compiler_params=_params("parallel", "parallel"),
        name="dilated_attention",
    )(proj_nat, proj_nat, proj_nat, proj_g1, proj_g1, proj_g1, proj_g2, proj_g2, proj_g2, proj_nat)


def kernel(x, norm_a, w_in_a, lam_q1, lam_k1, lam_q2, lam_k2, subln_a, w_out_a,
           norm_b, w_in_b, w_out_b, norm_f):
    batch, seq, d = x.shape
    m = batch * seq
    bf16 = jnp.bfloat16
    x2 = x.reshape(m, d)
    tn = 1024

    lam_init = 0.8 - 0.6 * math.exp(-0.3 * 0)
    q_scale_a = A_HEAD_DIM ** -0.5 * LOG2E
    wa = w_in_a[0]
    hn = _rmsnorm(x2, norm_a[0])
    v_blk0, v_blk1 = 2 * d // tn, 3 * d // tn
    col_map_a = lambda j: jnp.where(j < v_blk0, j, j + (v_blk1 - v_blk0))
    proj_a = _project(hn, wa.astype(bf16), 3 * d, col_map_a, 2 * A_HEAD_DIM, "in_proj_a",
                      n_scaled=d // tn, col_scale=q_scale_a, tn=tn)
    vt_a = _project_transposed(hn, wa[:, 2 * d:3 * d].T.astype(bf16), 2 * A_HEAD_DIM, "in_proj_a_vt")
    gated_a = _attention_a(proj_a, vt_a, lam_q1[0], lam_k1[0], lam_q2[0], lam_k2[0], subln_a[0],
                           batch, seq, lam_init)
    h1, hn1, hn4, hn16 = _out_project_a(gated_a, w_out_a[0].astype(bf16), x2, norm_b[0], batch, seq)

    q_scale_b = B_HEAD_DIM ** -0.5 * LOG2E
    w_in = w_in_b[0].astype(bf16)
    group_cols = 3 * d
    n_grp = group_cols // tn
    gate_blk0 = len(B_GROUPS) * n_grp
    col_map_0 = lambda j: jnp.where(j < n_grp, j, j + (gate_blk0 - n_grp))
    proj_nat = _project(hn1, w_in, group_cols + d, col_map_0, B_HEAD_DIM, "in_proj_b0",
                        n_scaled=d // tn, col_scale=q_scale_b, tn=tn)
    proj_g1 = _project(hn4.reshape(m, d), w_in, group_cols, lambda j: n_grp + j, B_HEAD_DIM,
                       "in_proj_b1", n_scaled=d // tn, col_scale=q_scale_b, tn=tn)
    proj_g2 = _project(hn16.reshape(m, d), w_in, group_cols, lambda j: 2 * n_grp + j, B_HEAD_DIM,
                       "in_proj_b2", n_scaled=d // tn, col_scale=q_scale_b, tn=tn)
    gated_b = _attention_b(proj_nat, proj_g1, proj_g2, batch, seq)
    out = _out_project_b(gated_b, w_out_b[0].astype(bf16), h1, norm_f)
    return out.reshape(batch, seq, d)
```

```python
import functools
import math

import jax
import jax.numpy as jnp
from jax import lax
from jax.experimental import pallas as pl
from jax.experimental.pallas import tpu as pltpu

A_HEAD_DIM = 128
B_GROUPS = ((128, 1), (512, 4), (2048, 16))
B_HEAD_DIM = 128
RMS_EPS = 1e-6
NEG_BIG = -1e30
LOG2E = math.log2(math.e)
LANES = 128

VMEM_LIMIT_BYTES = 48 * 1024 * 1024


def _params(*semantics):
    return pltpu.CompilerParams(dimension_semantics=semantics, vmem_limit_bytes=VMEM_LIMIT_BYTES)


def _alibi_slope_log2(h, n_heads):
    return LOG2E * jnp.exp2(jnp.zeros((1, 1), jnp.float32)
                            - (8.0 / n_heads) * (h + 1).astype(jnp.float32))


def _silu(z):
    return z * jax.nn.sigmoid(z)


def _aligned(start, multiple):
    return start if isinstance(start, int) else pl.multiple_of(start, multiple)


def _norm_kernel(x_ref, g_ref, o_ref):
    x = x_ref[...]
    ms = jnp.mean(x * x, axis=-1, keepdims=True)
    o_ref[...] = (x * lax.rsqrt(ms + RMS_EPS) * g_ref[...]).astype(o_ref.dtype)


def _rmsnorm(x, gain, tm=512):
    m, d = x.shape
    return pl.pallas_call(
        _norm_kernel,
        grid=(m // tm,),
        in_specs=[pl.BlockSpec((tm, d), lambda i: (i, 0)),
                  pl.BlockSpec((1, d), lambda i: (0, 0))],
        out_specs=pl.BlockSpec((tm, d), lambda i: (i, 0)),
        out_shape=jax.ShapeDtypeStruct((m, d), jnp.bfloat16),
        compiler_params=_params("parallel"),
        name="rmsnorm",
    )(x, gain.reshape(1, d))


def _proj_kernel(a_ref, w_ref, o_ref, w_sc, *, cb, n_scaled, col_scale):
    @pl.when(pl.program_id(1) == 0)
    def _():
        w_sc[...] = w_ref[...].astype(w_sc.dtype)

    acc = jnp.dot(a_ref[...], w_sc[...], preferred_element_type=jnp.float32)
    if n_scaled:
        acc = acc * jnp.where(pl.program_id(0) < n_scaled, col_scale, 1.0)
    for c in range(o_ref.shape[0]):
        o_ref[c] = acc[:, c * cb:(c + 1) * cb].astype(o_ref.dtype)


def _project(a, w, n_out, col_block_map, cb, name, n_scaled=0, col_scale=1.0, tm=1024, tn=1024):
    m, k = a.shape
    return pl.pallas_call(
        functools.partial(_proj_kernel, cb=cb, n_scaled=n_scaled, col_scale=col_scale),
        grid=(n_out // tn, m // tm),
        in_specs=[pl.BlockSpec((tm, k), lambda j, i: (i, 0)),
                  pl.BlockSpec((k, tn), lambda j, i: (0, col_block_map(j)))],
        out_specs=pl.BlockSpec((tn // cb, tm, cb), lambda j, i: (j, i, 0)),
        out_shape=jax.ShapeDtypeStruct((n_out // cb, m, cb), jnp.bfloat16),
        scratch_shapes=[pltpu.VMEM((k, tn), jnp.bfloat16)],
        compiler_params=_params("arbitrary", "arbitrary"),
        name=name,
    )(a, w)


def _proj_t_kernel(a_ref, w_ref, o_ref, wt_sc, *, cb):
    @pl.when(pl.program_id(1) == 0)
    def _():
        wt_sc[...] = w_ref[...].T.astype(wt_sc.dtype)

    acc = lax.dot_general(wt_sc[...], a_ref[...], (((1,), (1,)), ((), ())),
                          preferred_element_type=jnp.float32)
    for c in range(o_ref.shape[0]):
        o_ref[c] = acc[:, c * cb:(c + 1) * cb].astype(o_ref.dtype)


def _project_transposed(a, w, n_out, col_block0, cb, name, tm=1024, tn=1024):
    m, k = a.shape
    return pl.pallas_call(
        functools.partial(_proj_t_kernel, cb=cb),
        grid=(n_out // tn, m // tm),
        in_specs=[pl.BlockSpec((tm, k), lambda j, i: (i, 0)),
                  pl.BlockSpec((k, tn), lambda j, i: (0, col_block0 + j))],
        out_specs=pl.BlockSpec((tm // cb, tn, cb), lambda j, i: (i, j, 0)),
        out_shape=jax.ShapeDtypeStruct((m // cb, n_out, cb), jnp.bfloat16),
        scratch_shapes=[pltpu.VMEM((tn, k), jnp.bfloat16)],
        compiler_params=_params("arbitrary", "arbitrary"),
        name=name,
    )(a, w)


def _gather_slabs(a_ref):
    return jnp.concatenate([a_ref[c] for c in range(a_ref.shape[0])], axis=-1)


def _out_proj_a_kernel(a_ref, w_ref, r_ref, g_ref, h_ref, hn1_ref, hn4_ref, hn16_ref, slab_sc):
    y = jnp.dot(_gather_slabs(a_ref), w_ref[...], preferred_element_type=jnp.float32) + r_ref[...]
    h_ref[...] = y
    ms = jnp.mean(y * y, axis=-1, keepdims=True)
    hn = y * lax.rsqrt(ms + RMS_EPS) * g_ref[...]
    hn1_ref[...] = hn.astype(hn1_ref.dtype)
    tm, d = hn.shape
    n_slab = d // LANES
    for c in range(n_slab):
        slab_sc[c] = hn[:, c * LANES:(c + 1) * LANES]
    for ref in (hn4_ref, hn16_ref):
        dil = ref.shape[1]
        for r in range(dil):
            rows = jnp.concatenate(
                [slab_sc[c, pl.ds(r, tm // dil, stride=dil), :] for c in range(n_slab)], axis=-1)
            ref[0, r] = rows.astype(ref.dtype)


def _out_project_a(a, w, resid, gain, batch, seq, tm=256):
    nc, m, cb = a.shape
    k, n = w.shape
    nb = seq // tm
    const = lambda shape: pl.BlockSpec(shape, lambda i: (0,) * len(shape),
                                       pipeline_mode=pl.Buffered(1))
    perm_spec = lambda dil: pl.BlockSpec((1, dil, tm // dil, n), lambda i: (i // nb, 0, i % nb, 0))
    perm_shape = lambda dil: jax.ShapeDtypeStruct((batch, dil, seq // dil, n), jnp.bfloat16)
    return pl.pallas_call(
        _out_proj_a_kernel,
        grid=(m // tm,),
        in_specs=[pl.BlockSpec((nc, tm, cb), lambda i: (0, i, 0)),
                  const((k, n)),
                  pl.BlockSpec((tm, n), lambda i: (i, 0)),
                  const((1, n))],
        out_specs=[pl.BlockSpec((tm, n), lambda i: (i, 0)),
                   pl.BlockSpec((tm, n), lambda i: (i, 0)),
                   perm_spec(4), perm_spec(16)],
        out_shape=[jax.ShapeDtypeStruct((m, n), jnp.float32),
                   jax.ShapeDtypeStruct((m, n), jnp.bfloat16),
                   perm_shape(4), perm_shape(16)],
        scratch_shapes=[pltpu.VMEM((n // LANES, tm, LANES), jnp.float32)],
        compiler_params=_params("parallel"),
        name="out_proj_a",
    )(a, w, resid, gain.reshape(1, n))


def _out_proj_b_kernel(a_ref, w_ref, r_ref, g_ref, o_ref):
    y = jnp.dot(_gather_slabs(a_ref), w_ref[...], preferred_element_type=jnp.float32) + r_ref[...]
    ms = jnp.mean(y * y, axis=-1, keepdims=True)
    o_ref[...] = y * lax.rsqrt(ms + RMS_EPS) * g_ref[...]


def _out_project_b(a, w, resid, gain, tm=512):
    nc, m, cb = a.shape
    k, n = w.shape
    const = lambda shape: pl.BlockSpec(shape, lambda i: (0,) * len(shape),
                                       pipeline_mode=pl.Buffered(1))
    return pl.pallas_call(
        _out_proj_b_kernel,
        grid=(m // tm,),
        in_specs=[pl.BlockSpec((nc, tm, cb), lambda i: (0, i, 0)),
                  const((k, n)),
                  pl.BlockSpec((tm, n), lambda i: (i, 0)),
                  const((1, n))],
        out_specs=pl.BlockSpec((tm, n), lambda i: (i, 0)),
        out_shape=jax.ShapeDtypeStruct((m, n), jnp.float32),
        compiler_params=_params("parallel"),
        name="out_proj_b",
    )(a, w, resid, gain.reshape(1, n))


def _attn_a_kernel(q_ref, k_ref, vt_ref, z_ref, lq1_ref, lk1_ref, lq2_ref, lk2_ref, sub_ref,
                   o_ref, s_sc, bias_sc, m_sc, l_sc, acc_sc, *, tile, n_heads, lam_init):
    hd = A_HEAD_DIM
    seq = q_ref.shape[1]
    nt = seq // tile
    n_pairs = nt * (nt + 1) // 2
    h = pl.program_id(1)
    slope = _alibi_slope_log2(h, n_heads)
    rel0 = (lax.broadcasted_iota(jnp.int32, (tile, tile), 1)
            - lax.broadcasted_iota(jnp.int32, (tile, tile), 0))
    base = -slope * rel0.astype(jnp.float32)
    bias_sc[0] = base
    bias_sc[1] = jnp.where(rel0 >= 0, base, -jnp.inf)
    lam = (jnp.exp(jnp.sum(lq1_ref[...] * lk1_ref[...], axis=-1, keepdims=True))
           - jnp.exp(jnp.sum(lq2_ref[...] * lk2_ref[...], axis=-1, keepdims=True)) + lam_init)

    def reset():
        m_sc[...] = jnp.full(m_sc.shape, -jnp.inf, jnp.float32)
        l_sc[...] = jnp.zeros(l_sc.shape, jnp.float32)
        acc_sc[...] = jnp.zeros(acc_sc.shape, jnp.float32)

    def pair_ids(t):
        t = jnp.asarray(t, jnp.int32)
        qt = functools.reduce(lambda a, b: a + b,
                              [(t >= i * (i + 1) // 2).astype(jnp.int32) for i in range(1, nt)])
        return qt, t - qt * (qt + 1) // 2

    def scores(t, slot):
        qt, kt = pair_ids(t)
        q = q_ref[0, pl.ds(_aligned(qt * tile, tile), tile), :]
        kc = k_ref[0, pl.ds(_aligned(kt * tile, tile), tile), :]
        bias = bias_sc[(kt == qt).astype(jnp.int32)]
        for i in range(2):
            s_sc[slot, i] = lax.dot_general(
                kc[:, i * hd:(i + 1) * hd], q[:, i * hd:(i + 1) * hd], (((1,), (1,)), ((), ())),
                preferred_element_type=jnp.float32) + bias

    def accumulate(t, slot):
        qt, kt = pair_ids(t)
        delta = -slope * ((qt - kt) * tile).astype(jnp.float32)
        probs, alphas = [], []
        for i in range(2):
            s = s_sc[slot, i]
            m_prev = m_sc[i]
            m_new = jnp.maximum(m_prev, jnp.max(s, axis=0, keepdims=True) + delta)
            alpha = jnp.exp2(m_prev - m_new)
            p = jnp.exp2(s - (m_new - delta))
            l_sc[i] = alpha * l_sc[i] + jnp.sum(p, axis=0, keepdims=True)
            m_sc[i] = m_new
            alphas.append(alpha)
            probs.append(p.astype(vt_ref.dtype))
        vtc = vt_ref[kt]
        for i in range(2):
            acc_sc[i] = alphas[i] * acc_sc[i] + jnp.dot(vtc, probs[i],
                                                        preferred_element_type=jnp.float32)

        @pl.when(kt == qt)
        def _():
            rows = pl.ds(_aligned(qt * tile, tile), tile)
            o = (acc_sc[0] / l_sc[0] - lam * (acc_sc[1] / l_sc[1])).T
            ms = jnp.mean(o * o, axis=-1, keepdims=True)
            o = o * lax.rsqrt(ms + RMS_EPS) * sub_ref[...] * (1.0 - lam_init)
            o_ref[0, rows, :] = (o * _silu(z_ref[0, rows, :].astype(jnp.float32))).astype(o_ref.dtype)
            reset()

    reset()
    scores(0, 0)

    def two_pairs(it, carry):
        for u in range(2):
            t = 2 * it + u
            scores(jnp.minimum(t + 1, n_pairs - 1), 1 - u)
            accumulate(t, u)
        return carry

    lax.fori_loop(0, n_pairs // 2, two_pairs, 0)


def _attention_a(proj, vt, lam_q1, lam_k1, lam_q2, lam_k2, subln, batch, seq, lam_init, tile=512):
    w2 = 2 * A_HEAD_DIM
    nh = proj.shape[0] // 3
    nt = seq // tile
    assert (nt * (nt + 1) // 2) % 2 == 0 and vt.shape[2] == tile
    vec = lambda a: a.reshape(1, -1)
    vec_spec = pl.BlockSpec((1, A_HEAD_DIM), lambda b, h: (0, 0))
    slab = lambda off: pl.BlockSpec((1, seq, w2), lambda b, h: (off + h, b, 0))
    return pl.pallas_call(
        functools.partial(_attn_a_kernel, tile=tile, n_heads=nh, lam_init=lam_init),
        grid=(batch, nh),
        in_specs=[slab(0), slab(nh),
                  pl.BlockSpec((nt, w2, tile), lambda b, h: (b, h, 0)),
                  slab(2 * nh),
                  vec_spec, vec_spec, vec_spec, vec_spec,
                  pl.BlockSpec((1, w2), lambda b, h: (0, 0))],
        out_specs=slab(0),
        out_shape=jax.ShapeDtypeStruct((nh, batch * seq, w2), jnp.bfloat16),
        scratch_shapes=[pltpu.VMEM((2, 2, tile, tile), jnp.float32),
                        pltpu.VMEM((2, tile, tile), jnp.float32),
                        pltpu.VMEM((2, 1, tile), jnp.float32),
                        pltpu.VMEM((2, 1, tile), jnp.float32),
                        pltpu.VMEM((2, w2, tile), jnp.float32)],
        compiler_params=_params("parallel", "arbitrary"),
        name="diff_attention",
    )(proj, proj, vt, proj, vec(lam_q1), vec(lam_k1), vec(lam_q2), vec(lam_k2), vec(subln))


_B_BATCH = 8


def _attn_b_kernel(q0_ref, k0_ref, v0_ref, q1_ref, k1_ref, v1_ref, q2_ref, k2_ref, v2_ref, z_ref,
                   o_ref, out_sc, lse_sc, *, seq, n_heads, groups):
    hd = B_HEAD_DIM
    h = pl.program_id(1)
    slope = _alibi_slope_log2(h, n_heads)
    qkv = ((q0_ref, k0_ref, v0_ref), (q1_ref, k1_ref, v1_ref), (q2_ref, k2_ref, v2_ref))

    for g, (window, dil) in enumerate(groups):
        w = window // dil
        n_per = (seq // dil) // w
        q_ref, k_ref, v_ref = qkv[g]

        def band_bias(n_keys, w=w, dil=dil):
            ii = lax.broadcasted_iota(jnp.int32, (w, n_keys), 0)
            jj = lax.broadcasted_iota(jnp.int32, (w, n_keys), 1)
            rel = ii + (n_keys - w) - jj
            return jnp.where((rel >= 0) & (rel <= w),
                             -slope * (rel * dil).astype(jnp.float32), NEG_BIG)

        biases = {w: band_bias(w)}
        if n_per > 1:
            biases[2 * w] = band_bias(2 * w)
        n_blocks = seq // w
        for b0 in range(0, n_blocks, _B_BATCH):
            blocks = []
            for n in range(b0, min(b0 + _B_BATCH, n_blocks)):
                res, nb = divmod(n, n_per)
                n_keys = w if nb == 0 else 2 * w
                key_rows = pl.ds((n + 1) * w - n_keys, n_keys)
                q = q_ref[0, pl.ds(n * w, w), :]
                s = lax.dot_general(q, k_ref[0, key_rows, :], (((1,), (1,)), ((), ())),
                                    preferred_element_type=jnp.float32) + biases[n_keys]
                blocks.append((res, nb, key_rows, s))
            soft = []
            for res, nb, key_rows, s in blocks:
                m = jnp.max(s, axis=-1, keepdims=True)
                p = jnp.exp2(s - m)
                den = jnp.sum(p, axis=-1, keepdims=True)
                soft.append((p.astype(v_ref.dtype), den, m + jnp.log2(den)))
            for (res, nb, key_rows, _), (p, den, lse) in zip(blocks, soft):
                o = jnp.dot(p, v_ref[0, key_rows, :], preferred_element_type=jnp.float32) / den
                t0 = nb * (w * dil) + res
                dst = pl.ds(t0, w, stride=dil) if dil > 1 else pl.ds(t0, w)
                out_sc[g, dst, :] = o
                lse_sc[g, dst, :] = jnp.broadcast_to(lse, (w, hd))

    tc = 256

    def combine(c, carry):
        rows = pl.ds(pl.multiple_of(c * tc, tc), tc)
        lses = [lse_sc[g, rows, :] for g in range(len(groups))]
        top = functools.reduce(jnp.maximum, lses)
        wts = [jnp.exp2(l - top) for l in lses]
        num = functools.reduce(lambda a, b: a + b,
                               [wt * out_sc[g, rows, :] for g, wt in enumerate(wts)])
        o = num / functools.reduce(lambda a, b: a + b, wts)
        o_ref[0, rows, :] = (o * _silu(z_ref[0, rows, :].astype(jnp.float32))).astype(o_ref.dtype)
        return carry

    lax.fori_loop(0, seq // tc, combine, 0)


def _attention_b(proj_nat, proj_g1, proj_g2, batch, seq):
    hd = B_HEAD_DIM
    nh = proj_g1.shape[0] // 3
    assert all(seq % window == 0 for window, _ in B_GROUPS)
    slab = lambda off: pl.BlockSpec((1, seq, hd), lambda b, h: (off + h, b, 0))
    return pl.pallas_call(
        functools.partial(_attn_b_kernel, seq=seq, n_heads=nh, groups=B_GROUPS),
        grid=(batch, nh),
        in_specs=[slab(0), slab(nh), slab(2 * nh),
                  slab(0), slab(nh), slab(2 * nh),
                  slab(0), slab(nh), slab(2 * nh),
                  slab(3 * nh)],
        out_specs=pl.BlockSpec((1, seq, hd), lambda b, h: (h, b, 0)),
        out_shape=jax.ShapeDtypeStruct((nh, batch * seq, hd), jnp.bfloat16),
        scratch_shapes=[pltpu.VMEM((len(B_GROUPS), seq, hd), jnp.float32),
                        pltpu.VMEM((len(B_GROUPS), seq, hd), jnp.float32)],
        compiler_params=_params("parallel", "parallel"),
        name="dilated_attention",
    )(proj_nat, proj_nat, proj_nat, proj_g1, proj_g1, proj_g1, proj_g2, proj_g2, proj_g2, proj_nat)


def kernel(x, norm_a, w_in_a, lam_q1, lam_k1, lam_q2, lam_k2, subln_a, w_out_a,
           norm_b, w_in_b, w_out_b, norm_f):
    batch, seq, d = x.shape
    m = batch * seq
    bf16 = jnp.bfloat16
    x2 = x.reshape(m, d)
    tn = 1024
    a_tile = 512

    lam_init = 0.8 - 0.6 * math.exp(-0.3 * 0)
    q_scale_a = A_HEAD_DIM ** -0.5 * LOG2E
    hn = _rmsnorm(x2, norm_a[0])
    v_blk0, v_blk1 = 2 * d // tn, 3 * d // tn
    col_map_a = lambda j: jnp.where(j < v_blk0, j, j + (v_blk1 - v_blk0))
    proj_a = _project(hn, w_in_a[0], 3 * d, col_map_a, 2 * A_HEAD_DIM, "in_proj_a",
                      n_scaled=d // tn, col_scale=q_scale_a, tn=tn)
    vt_a = _project_transposed(hn, w_in_a[0], d, v_blk0, a_tile, "in_proj_a_vt", tn=tn)
    gated_a = _attention_a(proj_a, vt_a, lam_q1[0], lam_k1[0], lam_q2[0], lam_k2[0], subln_a[0],
                           batch, seq, lam_init, tile=a_tile)
    h1, hn1, hn4, hn16 = _out_project_a(gated_a, w_out_a[0].astype(bf16), x2, norm_b[0], batch, seq)

    q_scale_b = B_HEAD_DIM ** -0.5 * LOG2E
    w_in = w_in_b[0]
    group_cols = 3 * d
    n_grp = group_cols // tn
    gate_blk0 = len(B_GROUPS) * n_grp
    col_map_0 = lambda j: jnp.where(j < n_grp, j, j + (gate_blk0 - n_grp))
    proj_nat = _project(hn1, w_in, group_cols + d, col_map_0, B_HEAD_DIM, "in_proj_b0",
                        n_scaled=d // tn, col_scale=q_scale_b, tn=tn)
    proj_g1 = _project(hn4.reshape(m, d), w_in, group_cols, lambda j: n_grp + j, B_HEAD_DIM,
                       "in_proj_b1", n_scaled=d // tn, col_scale=q_scale_b, tn=tn)
    proj_g2 = _project(hn16.reshape(m, d), w_in, group_cols, lambda j: 2 * n_grp + j, B_HEAD_DIM,
                       "in_proj_b2", n_scaled=d // tn, col_scale=q_scale_b, tn=tn)
    gated_b = _attention_b(proj_nat, proj_g1, proj_g2, batch, seq)
    out = _out_project_b(gated_b, w_out_b[0].astype(bf16), h1, norm_f)
    return out.reshape(batch, seq, d)
```

```python
import functools
import math

import jax
import jax.numpy as jnp
from jax import lax
from jax.experimental import pallas as pl
from jax.experimental.pallas import tpu as pltpu

A_HEAD_DIM = 128
B_GROUPS = ((128, 1), (512, 4), (2048, 16))
B_HEAD_DIM = 128
RMS_EPS = 1e-6
NEG_BIG = -1e30
LOG2E = math.log2(math.e)
LANES = 128

VMEM_LIMIT_BYTES = 48 * 1024 * 1024


def _params(*semantics):
    return pltpu.CompilerParams(dimension_semantics=semantics, vmem_limit_bytes=VMEM_LIMIT_BYTES)


def _alibi_slope_log2(h, n_heads):
    return LOG2E * jnp.exp2(jnp.zeros((1, 1), jnp.float32)
                            - (8.0 / n_heads) * (h + 1).astype(jnp.float32))


def _silu(z):
    return z * jax.nn.sigmoid(z)


def _aligned(start, multiple):
    return start if isinstance(start, int) else pl.multiple_of(start, multiple)


def _norm_kernel(x_ref, g_ref, o_ref):
    x = x_ref[...]
    ms = jnp.mean(x * x, axis=-1, keepdims=True)
    o_ref[...] = (x * lax.rsqrt(ms + RMS_EPS) * g_ref[...]).astype(o_ref.dtype)


def _rmsnorm(x, gain, tm=512):
    m, d = x.shape
    return pl.pallas_call(
        _norm_kernel,
        grid=(m // tm,),
        in_specs=[pl.BlockSpec((tm, d), lambda i: (i, 0)),
                  pl.BlockSpec((1, d), lambda i: (0, 0))],
        out_specs=pl.BlockSpec((tm, d), lambda i: (i, 0)),
        out_shape=jax.ShapeDtypeStruct((m, d), jnp.bfloat16),
        compiler_params=_params("parallel"),
        name="rmsnorm",
    )(x, gain.reshape(1, d))


def _proj_kernel(a_ref, w_ref, o_ref, w_sc, *, cb, n_scaled, col_scale):
    @pl.when(pl.program_id(1) == 0)
    def _():
        w_sc[...] = w_ref[...].astype(w_sc.dtype)

    acc = jnp.dot(a_ref[...], w_sc[...], preferred_element_type=jnp.float32)
    if n_scaled:
        acc = acc * jnp.where(pl.program_id(0) < n_scaled, col_scale, 1.0)
    for c in range(o_ref.shape[0]):
        o_ref[c] = acc[:, c * cb:(c + 1) * cb].astype(o_ref.dtype)


def _project(a, w, n_out, col_block_map, cb, name, n_scaled=0, col_scale=1.0, tm=1024, tn=1024):
    m, k = a.shape
    return pl.pallas_call(
        functools.partial(_proj_kernel, cb=cb, n_scaled=n_scaled, col_scale=col_scale),
        grid=(n_out // tn, m // tm),
        in_specs=[pl.BlockSpec((tm, k), lambda j, i: (i, 0)),
                  pl.BlockSpec((k, tn), lambda j, i: (0, col_block_map(j)))],
        out_specs=pl.BlockSpec((tn // cb, tm, cb), lambda j, i: (j, i, 0)),
        out_shape=jax.ShapeDtypeStruct((n_out // cb, m, cb), jnp.bfloat16),
        scratch_shapes=[pltpu.VMEM((k, tn), jnp.bfloat16)],
        compiler_params=_params("arbitrary", "arbitrary"),
        name=name,
    )(a, w)


def _proj_t_kernel(a_ref, w_ref, o_ref, wt_sc, *, cb):
    @pl.when(pl.program_id(1) == 0)
    def _():
        wt_sc[...] = w_ref[...].T.astype(wt_sc.dtype)

    acc = lax.dot_general(wt_sc[...], a_ref[...], (((1,), (1,)), ((), ())),
                          preferred_element_type=jnp.float32)
    for c in range(o_ref.shape[0]):
        o_ref[c] = acc[:, c * cb:(c + 1) * cb].astype(o_ref.dtype)


def _project_transposed(a, w, n_out, col_block0, cb, name, tm=1024, tn=1024):
    m, k = a.shape
    return pl.pallas_call(
        functools.partial(_proj_t_kernel, cb=cb),
        grid=(n_out // tn, m // tm),
        in_specs=[pl.BlockSpec((tm, k), lambda j, i: (i, 0)),
                  pl.BlockSpec((k, tn), lambda j, i: (0, col_block0 + j))],
        out_specs=pl.BlockSpec((tm // cb, tn, cb), lambda j, i: (i, j, 0)),
        out_shape=jax.ShapeDtypeStruct((m // cb, n_out, cb), jnp.bfloat16),
        scratch_shapes=[pltpu.VMEM((tn, k), jnp.bfloat16)],
        compiler_params=_params("arbitrary", "arbitrary"),
        name=name,
    )(a, w)


def _gather_slabs(a_ref):
    return jnp.concatenate([a_ref[c] for c in range(a_ref.shape[0])], axis=-1)


def _out_proj_a_kernel(a_ref, w_ref, r_ref, g_ref, h_ref, hn1_ref, hn4_ref, hn16_ref, slab_sc):
    y = jnp.dot(_gather_slabs(a_ref), w_ref[...], preferred_element_type=jnp.float32) + r_ref[...]
    h_ref[...] = y
    ms = jnp.mean(y * y, axis=-1, keepdims=True)
    hn = y * lax.rsqrt(ms + RMS_EPS) * g_ref[...]
    hn1_ref[...] = hn.astype(hn1_ref.dtype)
    tm, d = hn.shape
    n_slab = d // LANES
    for c in range(n_slab):
        slab_sc[c] = hn[:, c * LANES:(c + 1) * LANES]
    for ref in (hn4_ref, hn16_ref):
        dil = ref.shape[1]
        for r in range(dil):
            rows = jnp.concatenate(
                [slab_sc[c, pl.ds(r, tm // dil, stride=dil), :] for c in range(n_slab)], axis=-1)
            ref[0, r] = rows.astype(ref.dtype)


def _out_project_a(a, w, resid, gain, batch, seq, tm=256):
    nc, m, cb = a.shape
    k, n = w.shape
    nb = seq // tm
    const = lambda shape: pl.BlockSpec(shape, lambda i: (0,) * len(shape),
                                       pipeline_mode=pl.Buffered(1))
    perm_spec = lambda dil: pl.BlockSpec((1, dil, tm // dil, n), lambda i: (i // nb, 0, i % nb, 0))
    perm_shape = lambda dil: jax.ShapeDtypeStruct((batch, dil, seq // dil, n), jnp.bfloat16)
    return pl.pallas_call(
        _out_proj_a_kernel,
        grid=(m // tm,),
        in_specs=[pl.BlockSpec((nc, tm, cb), lambda i: (0, i, 0)),
                  const((k, n)),
                  pl.BlockSpec((tm, n), lambda i: (i, 0)),
                  const((1, n))],
        out_specs=[pl.BlockSpec((tm, n), lambda i: (i, 0)),
                   pl.BlockSpec((tm, n), lambda i: (i, 0)),
                   perm_spec(4), perm_spec(16)],
        out_shape=[jax.ShapeDtypeStruct((m, n), jnp.float32),
                   jax.ShapeDtypeStruct((m, n), jnp.bfloat16),
                   perm_shape(4), perm_shape(16)],
        scratch_shapes=[pltpu.VMEM((n // LANES, tm, LANES), jnp.float32)],
        compiler_params=_params("parallel"),
        name="out_proj_a",
    )(a, w, resid, gain.reshape(1, n))


def _out_proj_b_kernel(a_ref, w_ref, r_ref, g_ref, o_ref):
    y = jnp.dot(_gather_slabs(a_ref), w_ref[...], preferred_element_type=jnp.float32) + r_ref[...]
    ms = jnp.mean(y * y, axis=-1, keepdims=True)
    o_ref[...] = y * lax.rsqrt(ms + RMS_EPS) * g_ref[...]


def _out_project_b(a, w, resid, gain, tm=512):
    nc, m, cb = a.shape
    k, n = w.shape
    const = lambda shape: pl.BlockSpec(shape, lambda i: (0,) * len(shape),
                                       pipeline_mode=pl.Buffered(1))
    return pl.pallas_call(
        _out_proj_b_kernel,
        grid=(m // tm,),
        in_specs=[pl.BlockSpec((nc, tm, cb), lambda i: (0, i, 0)),
                  const((k, n)),
                  pl.BlockSpec((tm, n), lambda i: (i, 0)),
                  const((1, n))],
        out_specs=pl.BlockSpec((tm, n), lambda i: (i, 0)),
        out_shape=jax.ShapeDtypeStruct((m, n), jnp.float32),
        compiler_params=_params("parallel"),
        name="out_proj_b",
    )(a, w, resid, gain.reshape(1, n))


def _attn_a_kernel(q_ref, k_ref, vt_ref, z_ref, lq1_ref, lk1_ref, lq2_ref, lk2_ref, sub_ref,
                   o_ref, s_sc, bias_sc, acc_sc, *, tile, n_heads, lam_init):
    hd = A_HEAD_DIM
    seq = q_ref.shape[1]
    nt = seq // tile
    h = pl.program_id(1)
    slope = _alibi_slope_log2(h, n_heads)
    rel0 = (lax.broadcasted_iota(jnp.int32, (tile, tile), 1)
            - lax.broadcasted_iota(jnp.int32, (tile, tile), 0))
    base = -slope * rel0.astype(jnp.float32)
    bias_sc[0] = base
    bias_sc[1] = jnp.where(rel0 >= 0, base, -jnp.inf)
    lam = (jnp.exp(jnp.sum(lq1_ref[...] * lk1_ref[...], axis=-1, keepdims=True))
           - jnp.exp(jnp.sum(lq2_ref[...] * lk2_ref[...], axis=-1, keepdims=True)) + lam_init)

    pairs = [(qt, kt) for qt in range(nt) for kt in range(qt + 1)]

    def scores(t):
        qt, kt = pairs[t]
        q = q_ref[0, pl.ds(qt * tile, tile), :]
        kc = k_ref[0, pl.ds(kt * tile, tile), :]
        bias = bias_sc[int(kt == qt)]
        for i in range(2):
            s = lax.dot_general(
                kc[:, i * hd:(i + 1) * hd], q[:, i * hd:(i + 1) * hd], (((1,), (1,)), ((), ())),
                preferred_element_type=jnp.float32) + bias
            s_sc[t % 2, i] = s
            col_max[t, i] = jnp.max(s, axis=0, keepdims=True)

    col_max = {}
    stats = {}

    def accumulate(t):
        qt, kt = pairs[t]
        delta = -slope * float((qt - kt) * tile)
        probs, alphas = [], []
        for i in range(2):
            m_new = col_max.pop((t, i)) + delta
            if kt > 0:
                m_prev, l_prev = stats[qt, i]
                m_new = jnp.maximum(m_prev, m_new)
                alphas.append(jnp.exp2(m_prev - m_new))
            p = jnp.exp2(s_sc[t % 2, i] - (m_new - delta))
            l_new = jnp.sum(p, axis=0, keepdims=True)
            if kt > 0:
                l_new = alphas[i] * l_prev + l_new
            stats[qt, i] = (m_new, l_new)
            probs.append(p.astype(vt_ref.dtype))
        vtc = vt_ref[kt]
        for i in range(2):
            pv = jnp.dot(vtc, probs[i], preferred_element_type=jnp.float32)
            acc_sc[qt, i] = pv if kt == 0 else alphas[i] * acc_sc[qt, i] + pv
        if kt == qt:
            rows = pl.ds(qt * tile, tile)
            o = (acc_sc[qt, 0] / stats[qt, 0][1]
                 - lam * (acc_sc[qt, 1] / stats[qt, 1][1])).T
            ms = jnp.mean(o * o, axis=-1, keepdims=True)
            o = o * lax.rsqrt(ms + RMS_EPS) * sub_ref[...] * (1.0 - lam_init)
            o_ref[0, rows, :] = (o * _silu(z_ref[0, rows, :].astype(jnp.float32))).astype(o_ref.dtype)

    scores(0)
    for t in range(len(pairs)):
        if t + 1 < len(pairs):
            scores(t + 1)
        accumulate(t)


def _attention_a(proj, vt, lam_q1, lam_k1, lam_q2, lam_k2, subln, batch, seq, lam_init, tile=512):
    w2 = 2 * A_HEAD_DIM
    nh = proj.shape[0] // 3
    nt = seq // tile
    assert vt.shape[2] == tile
    vec = lambda a: a.reshape(1, -1)
    vec_spec = pl.BlockSpec((1, A_HEAD_DIM), lambda b, h: (0, 0))
    slab = lambda off: pl.BlockSpec((1, seq, w2), lambda b, h: (off + h, b, 0))
    return pl.pallas_call(
        functools.partial(_attn_a_kernel, tile=tile, n_heads=nh, lam_init=lam_init),
        grid=(batch, nh),
        in_specs=[slab(0), slab(nh),
                  pl.BlockSpec((nt, w2, tile), lambda b, h: (b, h, 0)),
                  slab(2 * nh),
                  vec_spec, vec_spec, vec_spec, vec_spec,
                  pl.BlockSpec((1, w2), lambda b, h: (0, 0))],
        out_specs=slab(0),
        out_shape=jax.ShapeDtypeStruct((nh, batch * seq, w2), jnp.bfloat16),
        scratch_shapes=[pltpu.VMEM((2, 2, tile, tile), jnp.float32),
                        pltpu.VMEM((2, tile, tile), jnp.float32),
                        pltpu.VMEM((nt, 2, w2, tile), jnp.float32)],
        compiler_params=_params("parallel", "arbitrary"),
        name="diff_attention",
    )(proj, proj, vt, proj, vec(lam_q1), vec(lam_k1), vec(lam_q2), vec(lam_k2), vec(subln))


_B_BATCH = 8


def _attn_b_kernel(q0_ref, k0_ref, v0_ref, q1_ref, k1_ref, v1_ref, q2_ref, k2_ref, v2_ref, z_ref,
                   o_ref, out_sc, lse_sc, *, seq, n_heads, groups):
    hd = B_HEAD_DIM
    h = pl.program_id(1)
    slope = _alibi_slope_log2(h, n_heads)
    qkv = ((q0_ref, k0_ref, v0_ref), (q1_ref, k1_ref, v1_ref), (q2_ref, k2_ref, v2_ref))

    stages = []
    for g, (window, dil) in enumerate(groups):
        w = window // dil
        n_per = (seq // dil) // w
        q_ref, k_ref, v_ref = qkv[g]

        def band_bias(n_keys, w=w, dil=dil):
            ii = lax.broadcasted_iota(jnp.int32, (w, n_keys), 0)
            jj = lax.broadcasted_iota(jnp.int32, (w, n_keys), 1)
            rel = ii + (n_keys - w) - jj
            return jnp.where((rel >= 0) & (rel <= w),
                             -slope * (rel * dil).astype(jnp.float32), NEG_BIG)

        biases = {w: band_bias(w)}
        if n_per > 1:
            biases[2 * w] = band_bias(2 * w)
        n_blocks = seq // w

        def score_stage(b0, w=w, n_per=n_per, n_blocks=n_blocks, biases=biases,
                        q_ref=q_ref, k_ref=k_ref):
            blocks = []
            for n in range(b0, min(b0 + _B_BATCH, n_blocks)):
                res, nb = divmod(n, n_per)
                n_keys = w if nb == 0 else 2 * w
                key_rows = pl.ds((n + 1) * w - n_keys, n_keys)
                q = q_ref[0, pl.ds(n * w, w), :]
                s = lax.dot_general(q, k_ref[0, key_rows, :], (((1,), (1,)), ((), ())),
                                    preferred_element_type=jnp.float32) + biases[n_keys]
                blocks.append((res, nb, key_rows, s))
            return blocks

        def value_stage(blocks, w=w, dil=dil, g=g, v_ref=v_ref):
            soft = []
            for res, nb, key_rows, s in blocks:
                m = jnp.max(s, axis=-1, keepdims=True)
                p = jnp.exp2(s - m)
                den = jnp.sum(p, axis=-1, keepdims=True)
                soft.append((p.astype(v_ref.dtype), den, m + jnp.log2(den)))
            for (res, nb, key_rows, _), (p, den, lse) in zip(blocks, soft):
                o = jnp.dot(p, v_ref[0, key_rows, :], preferred_element_type=jnp.float32) / den
                t0 = nb * (w * dil) + res
                dst = pl.ds(t0, w, stride=dil) if dil > 1 else pl.ds(t0, w)
                out_sc[g, dst, :] = o
                lse_sc[g, dst, :] = jnp.broadcast_to(lse, (w, hd))

        stages += [(score_stage, b0, value_stage) for b0 in range(0, n_blocks, _B_BATCH)]

    for score_stage, b0, value_stage in stages:
        value_stage(score_stage(b0))

    tc = 256

    def combine(c, carry):
        rows = pl.ds(pl.multiple_of(c * tc, tc), tc)
        lses = [lse_sc[g, rows, :] for g in range(len(groups))]
        top = functools.reduce(jnp.maximum, lses)
        wts = [jnp.exp2(l - top) for l in lses]
        num = functools.reduce(lambda a, b: a + b,
                               [wt * out_sc[g, rows, :] for g, wt in enumerate(wts)])
        o = num / functools.reduce(lambda a, b: a + b, wts)
        o_ref[0, rows, :] = (o * _silu(z_ref[0, rows, :].astype(jnp.float32))).astype(o_ref.dtype)
        return carry

    lax.fori_loop(0, seq // tc, combine, 0)


def _attention_b(proj_nat, proj_g1, proj_g2, batch, seq):
    hd = B_HEAD_DIM
    nh = proj_g1.shape[0] // 3
    assert all(seq % window == 0 for window, _ in B_GROUPS)
    slab = lambda off: pl.BlockSpec((1, seq, hd), lambda b, h: (off + h, b, 0))
    return pl.pallas_call(
        functools.partial(_attn_b_kernel, seq=seq, n_heads=nh, groups=B_GROUPS),
        grid=(batch, nh),
        in_specs=[slab(0), slab(nh), slab(2 * nh),
                  slab(0), slab(nh), slab(2 * nh),
                  slab(0), slab(nh), slab(2 * nh),
                  slab(3 * nh)],
        out_specs=pl.BlockSpec((1, seq, hd), lambda b, h: (h, b, 0)),
        out_shape=jax.ShapeDtypeStruct((nh, batch * seq, hd), jnp.bfloat16),
        scratch_shapes=[pltpu.VMEM((len(B_GROUPS), seq, hd), jnp.float32),
                        pltpu.VMEM((len(B_GROUPS), seq, hd), jnp.float32)],
        compiler_params=_params("parallel", "parallel"),
        name="dilated_attention",
    )(proj_nat, proj_nat, proj_nat, proj_g1, proj_g1, proj_g1, proj_g2, proj_g2, proj_g2, proj_nat)


def kernel(x, norm_a, w_in_a, lam_q1, lam_k1, lam_q2, lam_k2, subln_a, w_out_a,
           norm_b, w_in_b, w_out_b, norm_f):
    batch, seq, d = x.shape
    m = batch * seq
    bf16 = jnp.bfloat16
    x2 = x.reshape(m, d)
    tn = 1024
    a_tile = 512

    lam_init = 0.8 - 0.6 * math.exp(-0.3 * 0)
    q_scale_a = A_HEAD_DIM ** -0.5 * LOG2E
    hn = _rmsnorm(x2, norm_a[0])
    v_blk0, v_blk1 = 2 * d // tn, 3 * d // tn
    col_map_a = lambda j: jnp.where(j < v_blk0, j, j + (v_blk1 - v_blk0))
    proj_a = _project(hn, w_in_a[0], 3 * d, col_map_a, 2 * A_HEAD_DIM, "in_proj_a",
                      n_scaled=d // tn, col_scale=q_scale_a, tn=tn)
    vt_a = _project_transposed(hn, w_in_a[0], d, v_blk0, a_tile, "in_proj_a_vt", tn=tn)
    gated_a = _attention_a(proj_a, vt_a, lam_q1[0], lam_k1[0], lam_q2[0], lam_k2[0], subln_a[0],
                           batch, seq, lam_init, tile=a_tile)
    h1, hn1, hn4, hn16 = _out_project_a(gated_a, w_out_a[0].astype(bf16), x2, norm_b[0], batch, seq)

    q_scale_b = B_HEAD_DIM ** -0.5 * LOG2E
    w_in = w_in_b[0]
    group_cols = 3 * d
    n_grp = group_cols // tn
    gate_blk0 = len(B_GROUPS) * n_grp
    col_map_0 = lambda j: jnp.where(j < n_grp, j, j + (gate_blk0 - n_grp))
    proj_nat = _project(hn1, w_in, group_cols + d, col_map_0, B_HEAD_DIM, "in_proj_b0",
                        n_scaled=d // tn, col_scale=q_scale_b, tn=tn)
    proj_g1 = _project(hn4.reshape(m, d), w_in, group_cols, lambda j: n_grp + j, B_HEAD_DIM,
                       "in_proj_b1", n_scaled=d // tn, col_scale=q_scale_b, tn=tn)
    proj_g2 = _project(hn16.reshape(m, d), w_in, group_cols, lambda j: 2 * n_grp + j, B_HEAD_DIM,
                       "in_proj_b2", n_scaled=d // tn, col_scale=q_scale_b, tn=tn)
    gated_b = _attention_b(proj_nat, proj_g1, proj_g2, batch, seq)
    out = _out_project_b(gated_b, w_out_b[0].astype(bf16), h1, norm_f)
    return out.reshape(batch, seq, d)
```

```python
import functools
import math

import jax
import jax.numpy as jnp
from jax import lax
from jax.experimental import pallas as pl
from jax.experimental.pallas import tpu as pltpu

A_HEAD_DIM = 128
B_GROUPS = ((128, 1), (512, 4), (2048, 16))
B_HEAD_DIM = 128
RMS_EPS = 1e-6
NEG_BIG = -1e30
LOG2E = math.log2(math.e)
LANES = 128

VMEM_LIMIT_BYTES = 48 * 1024 * 1024


def _params(*semantics):
    return pltpu.CompilerParams(dimension_semantics=semantics, vmem_limit_bytes=VMEM_LIMIT_BYTES)


def _alibi_slope_log2(h, n_heads):
    return LOG2E * jnp.exp2(jnp.zeros((1, 1), jnp.float32)
                            - (8.0 / n_heads) * (h + 1).astype(jnp.float32))


def _silu(z):
    return z * jax.nn.sigmoid(z)


def _aligned(start, multiple):
    return start if isinstance(start, int) else pl.multiple_of(start, multiple)


def _norm_kernel(x_ref, g_ref, o_ref):
    x = x_ref[...]
    ms = jnp.mean(x * x, axis=-1, keepdims=True)
    o_ref[...] = (x * lax.rsqrt(ms + RMS_EPS) * g_ref[...]).astype(o_ref.dtype)


def _rmsnorm(x, gain, tm=512):
    m, d = x.shape
    return pl.pallas_call(
        _norm_kernel,
        grid=(m // tm,),
        in_specs=[pl.BlockSpec((tm, d), lambda i: (i, 0)),
                  pl.BlockSpec((1, d), lambda i: (0, 0))],
        out_specs=pl.BlockSpec((tm, d), lambda i: (i, 0)),
        out_shape=jax.ShapeDtypeStruct((m, d), jnp.bfloat16),
        compiler_params=_params("parallel"),
        name="rmsnorm",
    )(x, gain.reshape(1, d))


def _proj_kernel(a_ref, w_ref, o_ref, w_sc, *, cb, n_scaled, col_scale):
    @pl.when(pl.program_id(1) == 0)
    def _():
        w_sc[...] = w_ref[...].astype(w_sc.dtype)

    acc = jnp.dot(a_ref[...], w_sc[...], preferred_element_type=jnp.float32)
    if n_scaled:
        acc = acc * jnp.where(pl.program_id(0) < n_scaled, col_scale, 1.0)
    for c in range(o_ref.shape[0]):
        o_ref[c] = acc[:, c * cb:(c + 1) * cb].astype(o_ref.dtype)


def _project(a, w, n_out, col_block_map, cb, name, n_scaled=0, col_scale=1.0, tm=1024, tn=1024):
    m, k = a.shape
    return pl.pallas_call(
        functools.partial(_proj_kernel, cb=cb, n_scaled=n_scaled, col_scale=col_scale),
        grid=(n_out // tn, m // tm),
        in_specs=[pl.BlockSpec((tm, k), lambda j, i: (i, 0)),
                  pl.BlockSpec((k, tn), lambda j, i: (0, col_block_map(j)))],
        out_specs=pl.BlockSpec((tn // cb, tm, cb), lambda j, i: (j, i, 0)),
        out_shape=jax.ShapeDtypeStruct((n_out // cb, m, cb), jnp.bfloat16),
        scratch_shapes=[pltpu.VMEM((k, tn), jnp.bfloat16)],
        compiler_params=_params("arbitrary", "arbitrary"),
        name=name,
    )(a, w)


def _proj_t_kernel(a_ref, w_ref, o_ref, wt_sc, *, cb):
    @pl.when(pl.program_id(1) == 0)
    def _():
        wt_sc[...] = w_ref[...].T.astype(wt_sc.dtype)

    acc = lax.dot_general(wt_sc[...], a_ref[...], (((1,), (1,)), ((), ())),
                          preferred_element_type=jnp.float32)
    for c in range(o_ref.shape[0]):
        o_ref[c] = acc[:, c * cb:(c + 1) * cb].astype(o_ref.dtype)


def _project_transposed(a, w, n_out, col_block0, cb, name, tm=1024, tn=1024):
    m, k = a.shape
    return pl.pallas_call(
        functools.partial(_proj_t_kernel, cb=cb),
        grid=(n_out // tn, m // tm),
        in_specs=[pl.BlockSpec((tm, k), lambda j, i: (i, 0)),
                  pl.BlockSpec((k, tn), lambda j, i: (0, col_block0 + j))],
        out_specs=pl.BlockSpec((tm // cb, tn, cb), lambda j, i: (i, j, 0)),
        out_shape=jax.ShapeDtypeStruct((m // cb, n_out, cb), jnp.bfloat16),
        scratch_shapes=[pltpu.VMEM((tn, k), jnp.bfloat16)],
        compiler_params=_params("arbitrary", "arbitrary"),
        name=name,
    )(a, w)


def _gather_slabs(a_ref):
    return jnp.concatenate([a_ref[c] for c in range(a_ref.shape[0])], axis=-1)


def _out_proj_a_kernel(a_ref, w_ref, r_ref, g_ref, h_ref, hn1_ref, hn4_ref, hn16_ref, slab_sc):
    y = jnp.dot(_gather_slabs(a_ref), w_ref[...], preferred_element_type=jnp.float32) + r_ref[...]
    h_ref[...] = y
    ms = jnp.mean(y * y, axis=-1, keepdims=True)
    hn = y * lax.rsqrt(ms + RMS_EPS) * g_ref[...]
    hn1_ref[...] = hn.astype(hn1_ref.dtype)
    tm, d = hn.shape
    n_slab = d // LANES
    for c in range(n_slab):
        slab_sc[c] = hn[:, c * LANES:(c + 1) * LANES]
    for ref in (hn4_ref, hn16_ref):
        dil = ref.shape[1]
        for r in range(dil):
            rows = jnp.concatenate(
                [slab_sc[c, pl.ds(r, tm // dil, stride=dil), :] for c in range(n_slab)], axis=-1)
            ref[0, r] = rows.astype(ref.dtype)


def _out_project_a(a, w, resid, gain, batch, seq, tm=256):
    nc, m, cb = a.shape
    k, n = w.shape
    nb = seq // tm
    const = lambda shape: pl.BlockSpec(shape, lambda i: (0,) * len(shape),
                                       pipeline_mode=pl.Buffered(1))
    perm_spec = lambda dil: pl.BlockSpec((1, dil, tm // dil, n), lambda i: (i // nb, 0, i % nb, 0))
    perm_shape = lambda dil: jax.ShapeDtypeStruct((batch, dil, seq // dil, n), jnp.bfloat16)
    return pl.pallas_call(
        _out_proj_a_kernel,
        grid=(m // tm,),
        in_specs=[pl.BlockSpec((nc, tm, cb), lambda i: (0, i, 0)),
                  const((k, n)),
                  pl.BlockSpec((tm, n), lambda i: (i, 0)),
                  const((1, n))],
        out_specs=[pl.BlockSpec((tm, n), lambda i: (i, 0)),
                   pl.BlockSpec((tm, n), lambda i: (i, 0)),
                   perm_spec(4), perm_spec(16)],
        out_shape=[jax.ShapeDtypeStruct((m, n), jnp.float32),
                   jax.ShapeDtypeStruct((m, n), jnp.bfloat16),
                   perm_shape(4), perm_shape(16)],
        scratch_shapes=[pltpu.VMEM((n // LANES, tm, LANES), jnp.float32)],
        compiler_params=_params("parallel"),
        name="out_proj_a",
    )(a, w, resid, gain.reshape(1, n))


def _out_proj_b_kernel(a_ref, w_ref, r_ref, g_ref, o_ref):
    y = jnp.dot(_gather_slabs(a_ref), w_ref[...], preferred_element_type=jnp.float32) + r_ref[...]
    ms = jnp.mean(y * y, axis=-1, keepdims=True)
    o_ref[...] = y * lax.rsqrt(ms + RMS_EPS) * g_ref[...]


def _out_project_b(a, w, resid, gain, tm=512):
    nc, m, cb = a.shape
    k, n = w.shape
    const = lambda shape: pl.BlockSpec(shape, lambda i: (0,) * len(shape),
                                       pipeline_mode=pl.Buffered(1))
    return pl.pallas_call(
        _out_proj_b_kernel,
        grid=(m // tm,),
        in_specs=[pl.BlockSpec((nc, tm, cb), lambda i: (0, i, 0)),
                  const((k, n)),
                  pl.BlockSpec((tm, n), lambda i: (i, 0)),
                  const((1, n))],
        out_specs=pl.BlockSpec((tm, n), lambda i: (i, 0)),
        out_shape=jax.ShapeDtypeStruct((m, n), jnp.float32),
        compiler_params=_params("parallel"),
        name="out_proj_b",
    )(a, w, resid, gain.reshape(1, n))


_SLOPE_PIECES = 3
_POS_RADIX = 64

def _attn_a_kernel(q_ref, k_ref, vt_ref, z_ref, lq1_ref, lk1_ref, lq2_ref, lk2_ref, sub_ref,
                   o_ref, s_sc, kfeat_sc, mask_sc, acc_sc, *, tile, n_heads, lam_init):
    hd = A_HEAD_DIM
    seq = q_ref.shape[1]
    nt = seq // tile
    h = pl.program_id(1)
    slope = _alibi_slope_log2(h, n_heads)
    lam = (jnp.exp(jnp.sum(lq1_ref[...] * lk1_ref[...], axis=-1, keepdims=True))
           - jnp.exp(jnp.sum(lq2_ref[...] * lk2_ref[...], axis=-1, keepdims=True)) + lam_init)

    bf16 = k_ref.dtype
    col = lax.broadcasted_iota(jnp.int32, (tile, hd), 1)
    row = lax.broadcasted_iota(jnp.int32, (tile, hd), 0)
    pieces, rest = [], slope
    for _ in range(_SLOPE_PIECES):
        piece = rest.astype(bf16).astype(jnp.float32)
        pieces.append(piece)
        rest = rest - piece
    q_feat = jnp.zeros((tile, hd), jnp.float32)
    for n, piece in enumerate(pieces):
        q_feat = jnp.where(col == n, _POS_RADIX * piece, q_feat)
        q_feat = jnp.where(col == _SLOPE_PIECES + n, piece, q_feat)
    q_feat = q_feat.astype(bf16)
    for kt in range(nt):
        pos = row + kt * tile
        hi = (pos // _POS_RADIX).astype(jnp.float32)
        lo = (pos % _POS_RADIX).astype(jnp.float32)
        kfeat_sc[kt] = jnp.where(col < _SLOPE_PIECES, hi,
                                 jnp.where(col < 2 * _SLOPE_PIECES, lo, 0.0)).astype(bf16)
    mask_sc[...] = jnp.where(lax.broadcasted_iota(jnp.int32, (tile, tile), 1)
                             >= lax.broadcasted_iota(jnp.int32, (tile, tile), 0), 0.0, -jnp.inf)

    pairs = [(qt, kt) for qt in range(nt) for kt in range(qt + 1)]

    def scores(t):
        qt, kt = pairs[t]
        q = q_ref[0, pl.ds(qt * tile, tile), :]
        kc = k_ref[0, pl.ds(kt * tile, tile), :]
        k_feat = kfeat_sc[kt]
        for i in range(2):
            s = lax.dot_general(
                jnp.concatenate([kc[:, i * hd:(i + 1) * hd], k_feat], axis=1),
                jnp.concatenate([q[:, i * hd:(i + 1) * hd], q_feat], axis=1),
                (((1,), (1,)), ((), ())), preferred_element_type=jnp.float32)
            if kt == qt:
                s = s + mask_sc[...]
            s_sc[t % 2, i] = s
            col_max[t, i] = jnp.max(s, axis=0, keepdims=True)

    col_max = {}
    stats = {}
    sub_gain = sub_ref[...] * (1.0 - lam_init)

    def accumulate(t):
        qt, kt = pairs[t]
        probs, alphas = [], []
        for i in range(2):
            m_new = col_max.pop((t, i))
            if kt > 0:
                m_prev, l_prev = stats[qt, i]
                m_new = jnp.maximum(m_prev, m_new)
                alphas.append(jnp.exp2(m_prev - m_new))
            p = jnp.exp2(s_sc[t % 2, i] - m_new)
            l_new = jnp.sum(p, axis=0, keepdims=True)
            if kt > 0:
                l_new = alphas[i] * l_prev + l_new
            stats[qt, i] = (m_new, l_new)
            probs.append(p.astype(vt_ref.dtype))
        vtc = vt_ref[kt]
        for i in range(2):
            pv = jnp.dot(vtc, probs[i], preferred_element_type=jnp.float32)
            acc_sc[qt, i] = pv if kt == 0 else alphas[i] * acc_sc[qt, i] + pv
        if kt == qt:
            rows = pl.ds(qt * tile, tile)
            o = (acc_sc[qt, 0] / stats[qt, 0][1]
                 - lam * (acc_sc[qt, 1] / stats[qt, 1][1])).T
            ms = jnp.mean(o * o, axis=-1, keepdims=True)
            o = o * lax.rsqrt(ms + RMS_EPS) * sub_gain
            o_ref[0, rows, :] = (o * _silu(z_ref[0, rows, :].astype(jnp.float32))).astype(o_ref.dtype)

    scores(0)
    for t in range(len(pairs)):
        if t + 1 < len(pairs):
            scores(t + 1)
        accumulate(t)


def _attention_a(proj, vt, lam_q1, lam_k1, lam_q2, lam_k2, subln, batch, seq, lam_init, tile=512):
    w2 = 2 * A_HEAD_DIM
    nh = proj.shape[0] // 3
    nt = seq // tile
    assert vt.shape[2] == tile
    vec = lambda a: a.reshape(1, -1)
    vec_spec = pl.BlockSpec((1, A_HEAD_DIM), lambda b, h: (0, 0))
    slab = lambda off: pl.BlockSpec((1, seq, w2), lambda b, h: (off + h, b, 0))
    return pl.pallas_call(
        functools.partial(_attn_a_kernel, tile=tile, n_heads=nh, lam_init=lam_init),
        grid=(batch, nh),
        in_specs=[slab(0), slab(nh),
                  pl.BlockSpec((nt, w2, tile), lambda b, h: (b, h, 0)),
                  slab(2 * nh),
                  vec_spec, vec_spec, vec_spec, vec_spec,
                  pl.BlockSpec((1, w2), lambda b, h: (0, 0))],
        out_specs=slab(0),
        out_shape=jax.ShapeDtypeStruct((nh, batch * seq, w2), jnp.bfloat16),
        scratch_shapes=[pltpu.VMEM((2, 2, tile, tile), jnp.float32),
                        pltpu.VMEM((nt, tile, A_HEAD_DIM), jnp.bfloat16),
                        pltpu.VMEM((tile, tile), jnp.float32),
                        pltpu.VMEM((nt, 2, w2, tile), jnp.float32)],
        compiler_params=_params("parallel", "arbitrary"),
        name="diff_attention",
    )(proj, proj, vt, proj, vec(lam_q1), vec(lam_k1), vec(lam_q2), vec(lam_k2), vec(subln))


_B_BATCH = 8


def _attn_b_kernel(q0_ref, k0_ref, v0_ref, q1_ref, k1_ref, v1_ref, q2_ref, k2_ref, v2_ref, z_ref,
                   o_ref, out_sc, lse_sc, *, seq, n_heads, groups):
    hd = B_HEAD_DIM
    h = pl.program_id(1)
    slope = _alibi_slope_log2(h, n_heads)
    qkv = ((q0_ref, k0_ref, v0_ref), (q1_ref, k1_ref, v1_ref), (q2_ref, k2_ref, v2_ref))

    stages = []
    for g, (window, dil) in enumerate(groups):
        w = window // dil
        n_per = (seq // dil) // w
        q_ref, k_ref, v_ref = qkv[g]

        def band_bias(n_keys, w=w, dil=dil):
            ii = lax.broadcasted_iota(jnp.int32, (w, n_keys), 0)
            jj = lax.broadcasted_iota(jnp.int32, (w, n_keys), 1)
            rel = ii + (n_keys - w) - jj
            return jnp.where((rel >= 0) & (rel <= w),
                             -slope * (rel * dil).astype(jnp.float32), NEG_BIG)

        biases = {w: band_bias(w)}
        if n_per > 1:
            biases[2 * w] = band_bias(2 * w)
        n_blocks = seq // w

        def score_stage(b0, w=w, n_per=n_per, n_blocks=n_blocks, biases=biases,
                        q_ref=q_ref, k_ref=k_ref):
            blocks = []
            for n in range(b0, min(b0 + _B_BATCH, n_blocks)):
                res, nb = divmod(n, n_per)
                n_keys = w if nb == 0 else 2 * w
                key_rows = pl.ds((n + 1) * w - n_keys, n_keys)
                q = q_ref[0, pl.ds(n * w, w), :]
                s = lax.dot_general(q, k_ref[0, key_rows, :], (((1,), (1,)), ((), ())),
                                    preferred_element_type=jnp.float32) + biases[n_keys]
                blocks.append((res, nb, key_rows, s))
            return blocks

        def value_stage(blocks, w=w, dil=dil, g=g, v_ref=v_ref):
            soft = []
            for res, nb, key_rows, s in blocks:
                m = jnp.max(s, axis=-1, keepdims=True)
                p = jnp.exp2(s - m)
                den = jnp.sum(p, axis=-1, keepdims=True)
                soft.append((p.astype(v_ref.dtype), den, m + jnp.log2(den)))
            for (res, nb, key_rows, _), (p, den, lse) in zip(blocks, soft):
                o = jnp.dot(p, v_ref[0, key_rows, :], preferred_element_type=jnp.float32) / den
                t0 = nb * (w * dil) + res
                dst = pl.ds(t0, w, stride=dil) if dil > 1 else pl.ds(t0, w)
                out_sc[g, dst, :] = o
                lse_sc[g, dst, :] = jnp.broadcast_to(lse, (w, hd))

        stages += [(score_stage, b0, value_stage) for b0 in range(0, n_blocks, _B_BATCH)]

    for score_stage, b0, value_stage in stages:
        value_stage(score_stage(b0))

    tc = 256

    def combine(c, carry):
        rows = pl.ds(pl.multiple_of(c * tc, tc), tc)
        lses = [lse_sc[g, rows, :] for g in range(len(groups))]
        top = functools.reduce(jnp.maximum, lses)
        wts = [jnp.exp2(l - top) for l in lses]
        num = functools.reduce(lambda a, b: a + b,
                               [wt * out_sc[g, rows, :] for g, wt in enumerate(wts)])
        o = num / functools.reduce(lambda a, b: a + b, wts)
        o_ref[0, rows, :] = (o * _silu(z_ref[0, rows, :].astype(jnp.float32))).astype(o_ref.dtype)
        return carry

    lax.fori_loop(0, seq // tc, combine, 0)


def _attention_b(proj_nat, proj_g1, proj_g2, batch, seq):
    hd = B_HEAD_DIM
    nh = proj_g1.shape[0] // 3
    assert all(seq % window == 0 for window, _ in B_GROUPS)
    slab = lambda off: pl.BlockSpec((1, seq, hd), lambda b, h: (off + h, b, 0))
    return pl.pallas_call(
        functools.partial(_attn_b_kernel, seq=seq, n_heads=nh, groups=B_GROUPS),
        grid=(batch, nh),
        in_specs=[slab(0), slab(nh), slab(2 * nh),
                  slab(0), slab(nh), slab(2 * nh),
                  slab(0), slab(nh), slab(2 * nh),
                  slab(3 * nh)],
        out_specs=pl.BlockSpec((1, seq, hd), lambda b, h: (h, b, 0)),
        out_shape=jax.ShapeDtypeStruct((nh, batch * seq, hd), jnp.bfloat16),
        scratch_shapes=[pltpu.VMEM((len(B_GROUPS), seq, hd), jnp.float32),
                        pltpu.VMEM((len(B_GROUPS), seq, hd), jnp.float32)],
        compiler_params=_params("parallel", "parallel"),
        name="dilated_attention",
    )(proj_nat, proj_nat, proj_nat, proj_g1, proj_g1, proj_g1, proj_g2, proj_g2, proj_g2, proj_nat)


def kernel(x, norm_a, w_in_a, lam_q1, lam_k1, lam_q2, lam_k2, subln_a, w_out_a,
           norm_b, w_in_b, w_out_b, norm_f):
    batch, seq, d = x.shape
    m = batch * seq
    bf16 = jnp.bfloat16
    x2 = x.reshape(m, d)
    tn = 1024
    a_tile = 512

    lam_init = 0.8 - 0.6 * math.exp(-0.3 * 0)
    q_scale_a = A_HEAD_DIM ** -0.5 * LOG2E
    hn = _rmsnorm(x2, norm_a[0])
    v_blk0, v_blk1 = 2 * d // tn, 3 * d // tn
    col_map_a = lambda j: jnp.where(j < v_blk0, j, j + (v_blk1 - v_blk0))
    proj_a = _project(hn, w_in_a[0], 3 * d, col_map_a, 2 * A_HEAD_DIM, "in_proj_a",
                      n_scaled=d // tn, col_scale=q_scale_a, tn=tn)
    vt_a = _project_transposed(hn, w_in_a[0], d, v_blk0, a_tile, "in_proj_a_vt", tn=tn)
    gated_a = _attention_a(proj_a, vt_a, lam_q1[0], lam_k1[0], lam_q2[0], lam_k2[0], subln_a[0],
                           batch, seq, lam_init, tile=a_tile)
    h1, hn1, hn4, hn16 = _out_project_a(gated_a, w_out_a[0].astype(bf16), x2, norm_b[0], batch, seq)

    q_scale_b = B_HEAD_DIM ** -0.5 * LOG2E
    w_in = w_in_b[0]
    group_cols = 3 * d
    n_grp = group_cols // tn
    gate_blk0 = len(B_GROUPS) * n_grp
    col_map_0 = lambda j: jnp.where(j < n_grp, j, j + (gate_blk0 - n_grp))
    proj_nat = _project(hn1, w_in, group_cols + d, col_map_0, B_HEAD_DIM, "in_proj_b0",
                        n_scaled=d // tn, col_scale=q_scale_b, tn=tn)
    proj_g1 = _project(hn4.reshape(m, d), w_in, group_cols, lambda j: n_grp + j, B_HEAD_DIM,
                       "in_proj_b1", n_scaled=d // tn, col_scale=q_scale_b, tn=tn)
    proj_g2 = _project(hn16.reshape(m, d), w_in, group_cols, lambda j: 2 * n_grp + j, B_HEAD_DIM,
                       "in_proj_b2", n_scaled=d // tn, col_scale=q_scale_b, tn=tn)
    gated_b = _attention_b(proj_nat, proj_g1, proj_g2, batch, seq)
    out = _out_project_b(gated_b, w_out_b[0].astype(bf16), h1, norm_f)
    return out.reshape(batch, seq, d)
```

```python
import functools
import math

import jax
import jax.numpy as jnp
from jax import lax
from jax.experimental import pallas as pl
from jax.experimental.pallas import tpu as pltpu

A_HEAD_DIM = 128
B_GROUPS = ((128, 1), (512, 4), (2048, 16))
B_HEAD_DIM = 128
RMS_EPS = 1e-6
NEG_BIG = -1e30
LOG2E = math.log2(math.e)
LANES = 128

VMEM_LIMIT_BYTES = 48 * 1024 * 1024


def _params(*semantics):
    return pltpu.CompilerParams(dimension_semantics=semantics, vmem_limit_bytes=VMEM_LIMIT_BYTES)


def _alibi_slope_log2(h, n_heads):
    return LOG2E * jnp.exp2(jnp.zeros((1, 1), jnp.float32)
                            - (8.0 / n_heads) * (h + 1).astype(jnp.float32))


def _silu(z):
    return z * jax.nn.sigmoid(z)


def _aligned(start, multiple):
    return start if isinstance(start, int) else pl.multiple_of(start, multiple)


def _norm_kernel(x_ref, g_ref, o_ref):
    x = x_ref[...]
    ms = jnp.mean(x * x, axis=-1, keepdims=True)
    o_ref[...] = (x * lax.rsqrt(ms + RMS_EPS) * g_ref[...]).astype(o_ref.dtype)


def _rmsnorm(x, gain, tm=512):
    m, d = x.shape
    return pl.pallas_call(
        _norm_kernel,
        grid=(m // tm,),
        in_specs=[pl.BlockSpec((tm, d), lambda i: (i, 0)),
                  pl.BlockSpec((1, d), lambda i: (0, 0))],
        out_specs=pl.BlockSpec((tm, d), lambda i: (i, 0)),
        out_shape=jax.ShapeDtypeStruct((m, d), jnp.bfloat16),
        compiler_params=_params("parallel"),
        name="rmsnorm",
    )(x, gain.reshape(1, d))


def _proj_kernel(a_ref, w_ref, o_ref, w_sc, *, cb, n_scaled, col_scale):
    @pl.when(pl.program_id(1) == 0)
    def _():
        w_sc[...] = w_ref[...].astype(w_sc.dtype)

    acc = jnp.dot(a_ref[...], w_sc[...], preferred_element_type=jnp.float32)
    if n_scaled:
        acc = acc * jnp.where(pl.program_id(0) < n_scaled, col_scale, 1.0)
    for c in range(o_ref.shape[0]):
        o_ref[c] = acc[:, c * cb:(c + 1) * cb].astype(o_ref.dtype)


def _project(a, w, n_out, col_block_map, cb, name, n_scaled=0, col_scale=1.0, tm=1024, tn=1024):
    m, k = a.shape
    return pl.pallas_call(
        functools.partial(_proj_kernel, cb=cb, n_scaled=n_scaled, col_scale=col_scale),
        grid=(n_out // tn, m // tm),
        in_specs=[pl.BlockSpec((tm, k), lambda j, i: (i, 0)),
                  pl.BlockSpec((k, tn), lambda j, i: (0, col_block_map(j)))],
        out_specs=pl.BlockSpec((tn // cb, tm, cb), lambda j, i: (j, i, 0)),
        out_shape=jax.ShapeDtypeStruct((n_out // cb, m, cb), jnp.bfloat16),
        scratch_shapes=[pltpu.VMEM((k, tn), jnp.bfloat16)],
        compiler_params=_params("arbitrary", "arbitrary"),
        name=name,
    )(a, w)


def _proj_t_kernel(a_ref, w_ref, o_ref, wt_sc, *, cb):
    @pl.when(pl.program_id(1) == 0)
    def _():
        wt_sc[...] = w_ref[...].T.astype(wt_sc.dtype)

    acc = lax.dot_general(wt_sc[...], a_ref[...], (((1,), (1,)), ((), ())),
                          preferred_element_type=jnp.float32)
    for c in range(o_ref.shape[0]):
        o_ref[c] = acc[:, c * cb:(c + 1) * cb].astype(o_ref.dtype)


def _project_transposed(a, w, n_out, col_block0, cb, name, tm=1024, tn=1024):
    m, k = a.shape
    return pl.pallas_call(
        functools.partial(_proj_t_kernel, cb=cb),
        grid=(n_out // tn, m // tm),
        in_specs=[pl.BlockSpec((tm, k), lambda j, i: (i, 0)),
                  pl.BlockSpec((k, tn), lambda j, i: (0, col_block0 + j))],
        out_specs=pl.BlockSpec((tm // cb, tn, cb), lambda j, i: (i, j, 0)),
        out_shape=jax.ShapeDtypeStruct((m // cb, n_out, cb), jnp.bfloat16),
        scratch_shapes=[pltpu.VMEM((tn, k), jnp.bfloat16)],
        compiler_params=_params("arbitrary", "arbitrary"),
        name=name,
    )(a, w)


def _gather_slabs(a_ref):
    return jnp.concatenate([a_ref[c] for c in range(a_ref.shape[0])], axis=-1)


def _out_proj_a_kernel(a_ref, w_ref, r_ref, g_ref, h_ref, hn1_ref, hn4_ref, hn16_ref, slab_sc):
    y = jnp.dot(_gather_slabs(a_ref), w_ref[...], preferred_element_type=jnp.float32) + r_ref[...]
    h_ref[...] = y
    ms = jnp.mean(y * y, axis=-1, keepdims=True)
    hn = y * lax.rsqrt(ms + RMS_EPS) * g_ref[...]
    hn1_ref[...] = hn.astype(hn1_ref.dtype)
    tm, d = hn.shape
    n_slab = d // LANES
    for c in range(n_slab):
        slab_sc[c] = hn[:, c * LANES:(c + 1) * LANES]
    for ref in (hn4_ref, hn16_ref):
        dil = ref.shape[1]
        for r in range(dil):
            rows = jnp.concatenate(
                [slab_sc[c, pl.ds(r, tm // dil, stride=dil), :] for c in range(n_slab)], axis=-1)
            ref[0, r] = rows.astype(ref.dtype)


def _out_project_a(a, w, resid, gain, batch, seq, tm=256):
    nc, m, cb = a.shape
    k, n = w.shape
    nb = seq // tm
    const = lambda shape: pl.BlockSpec(shape, lambda i: (0,) * len(shape),
                                       pipeline_mode=pl.Buffered(1))
    perm_spec = lambda dil: pl.BlockSpec((1, dil, tm // dil, n), lambda i: (i // nb, 0, i % nb, 0))
    perm_shape = lambda dil: jax.ShapeDtypeStruct((batch, dil, seq // dil, n), jnp.bfloat16)
    return pl.pallas_call(
        _out_proj_a_kernel,
        grid=(m // tm,),
        in_specs=[pl.BlockSpec((nc, tm, cb), lambda i: (0, i, 0)),
                  const((k, n)),
                  pl.BlockSpec((tm, n), lambda i: (i, 0)),
                  const((1, n))],
        out_specs=[pl.BlockSpec((tm, n), lambda i: (i, 0)),
                   pl.BlockSpec((tm, n), lambda i: (i, 0)),
                   perm_spec(4), perm_spec(16)],
        out_shape=[jax.ShapeDtypeStruct((m, n), jnp.float32),
                   jax.ShapeDtypeStruct((m, n), jnp.bfloat16),
                   perm_shape(4), perm_shape(16)],
        scratch_shapes=[pltpu.VMEM((n // LANES, tm, LANES), jnp.float32)],
        compiler_params=_params("parallel"),
        name="out_proj_a",
    )(a, w, resid, gain.reshape(1, n))


def _out_proj_b_kernel(a_ref, w_ref, r_ref, g_ref, o_ref):
    y = jnp.dot(_gather_slabs(a_ref), w_ref[...], preferred_element_type=jnp.float32) + r_ref[...]
    ms = jnp.mean(y * y, axis=-1, keepdims=True)
    o_ref[...] = y * lax.rsqrt(ms + RMS_EPS) * g_ref[...]


def _out_project_b(a, w, resid, gain, tm=512):
    nc, m, cb = a.shape
    k, n = w.shape
    const = lambda shape: pl.BlockSpec(shape, lambda i: (0,) * len(shape),
                                       pipeline_mode=pl.Buffered(1))
    return pl.pallas_call(
        _out_proj_b_kernel,
        grid=(m // tm,),
        in_specs=[pl.BlockSpec((nc, tm, cb), lambda i: (0, i, 0)),
                  const((k, n)),
                  pl.BlockSpec((tm, n), lambda i: (i, 0)),
                  const((1, n))],
        out_specs=pl.BlockSpec((tm, n), lambda i: (i, 0)),
        out_shape=jax.ShapeDtypeStruct((m, n), jnp.float32),
        compiler_params=_params("parallel"),
        name="out_proj_b",
    )(a, w, resid, gain.reshape(1, n))


_SLOPE_PIECES = 3
_POS_RADIX = 64

def _attn_a_kernel(q_ref, k_ref, vt_ref, z_ref, lq1_ref, lk1_ref, lq2_ref, lk2_ref, sub_ref,
                   o_ref, s_sc, kfeat_sc, mask_sc, acc_sc, *, tile, n_heads, lam_init):
    hd = A_HEAD_DIM
    seq = q_ref.shape[1]
    nt = seq // tile
    h = pl.program_id(1)
    slope = _alibi_slope_log2(h, n_heads)
    lam = (jnp.exp(jnp.sum(lq1_ref[...] * lk1_ref[...], axis=-1, keepdims=True))
           - jnp.exp(jnp.sum(lq2_ref[...] * lk2_ref[...], axis=-1, keepdims=True)) + lam_init)

    bf16 = k_ref.dtype
    col = lax.broadcasted_iota(jnp.int32, (tile, hd), 1)
    row = lax.broadcasted_iota(jnp.int32, (tile, hd), 0)
    pieces, rest = [], slope
    for _ in range(_SLOPE_PIECES):
        piece = rest.astype(bf16).astype(jnp.float32)
        pieces.append(piece)
        rest = rest - piece
    q_feat = jnp.zeros((tile, hd), jnp.float32)
    for n, piece in enumerate(pieces):
        q_feat = jnp.where(col == n, _POS_RADIX * piece, q_feat)
        q_feat = jnp.where(col == _SLOPE_PIECES + n, piece, q_feat)
    q_feat = q_feat.astype(bf16)
    for kt in range(nt):
        pos = row + kt * tile
        hi = (pos // _POS_RADIX).astype(jnp.float32)
        lo = (pos % _POS_RADIX).astype(jnp.float32)
        kfeat_sc[kt] = jnp.where(col < _SLOPE_PIECES, hi,
                                 jnp.where(col < 2 * _SLOPE_PIECES, lo, 0.0)).astype(bf16)
    half = tile // 2
    mask_sc[...] = jnp.where(lax.broadcasted_iota(jnp.int32, (half, half), 1)
                             >= lax.broadcasted_iota(jnp.int32, (half, half), 0), 0.0, -jnp.inf)

    pairs = [(qt, kt) for qt in range(nt) for kt in range(qt + 1)]

    def panels(qt, kt):
        if kt < qt:
            return [((0, tile), (0, tile), None)]
        return [((0, half), (0, half), 0), ((0, tile), (half, tile), half)]

    def scores(t):
        qt, kt = pairs[t]
        q = q_ref[0, pl.ds(qt * tile, tile), :]
        kc = k_ref[0, pl.ds(kt * tile, tile), :]
        k_feat = kfeat_sc[kt]
        for i in range(2):
            k_aug = jnp.concatenate([kc[:, i * hd:(i + 1) * hd], k_feat], axis=1)
            q_aug = jnp.concatenate([q[:, i * hd:(i + 1) * hd], q_feat], axis=1)
            maxima = []
            for (k0, k1), (c0, c1), mask_row in panels(qt, kt):
                s = lax.dot_general(k_aug[k0:k1], q_aug[c0:c1], (((1,), (1,)), ((), ())),
                                    preferred_element_type=jnp.float32)
                if mask_row == 0:
                    s = s + mask_sc[...]
                elif mask_row is not None:
                    s = jnp.concatenate([s[:mask_row], s[mask_row:] + mask_sc[...]], axis=0)
                s_sc[t % 2, i, k0:k1, c0:c1] = s
                maxima.append(jnp.max(s, axis=0, keepdims=True))
            col_max[t, i] = jnp.concatenate(maxima, axis=1)

    col_max = {}
    stats = {}
    sub_gain = sub_ref[...] * (1.0 - lam_init)

    def accumulate(t):
        qt, kt = pairs[t]
        probs, alphas = [], []
        for i in range(2):
            m_new = col_max.pop((t, i))
            if kt > 0:
                m_prev, l_prev = stats[qt, i]
                m_new = jnp.maximum(m_prev, m_new)
                alphas.append(jnp.exp2(m_prev - m_new))
            ps, sums = [], []
            for (k0, k1), (c0, c1), _ in panels(qt, kt):
                p = jnp.exp2(s_sc[t % 2, i, k0:k1, c0:c1] - m_new[:, c0:c1])
                sums.append(jnp.sum(p, axis=0, keepdims=True))
                ps.append(p.astype(vt_ref.dtype))
            l_new = jnp.concatenate(sums, axis=1)
            if kt > 0:
                l_new = alphas[i] * l_prev + l_new
            stats[qt, i] = (m_new, l_new)
            probs.append(ps)
        vtc = vt_ref[kt]
        for i in range(2):
            pv = jnp.concatenate(
                [jnp.dot(vtc[:, k0:k1], p, preferred_element_type=jnp.float32)
                 for ((k0, k1), _, _), p in zip(panels(qt, kt), probs[i])], axis=1)
            acc_sc[qt, i] = pv if kt == 0 else alphas[i] * acc_sc[qt, i] + pv
        if kt == qt:
            rows = pl.ds(qt * tile, tile)
            o = (acc_sc[qt, 0] / stats[qt, 0][1]
                 - lam * (acc_sc[qt, 1] / stats[qt, 1][1])).T
            ms = jnp.mean(o * o, axis=-1, keepdims=True)
            o = o * lax.rsqrt(ms + RMS_EPS) * sub_gain
            o_ref[0, rows, :] = (o * _silu(z_ref[0, rows, :].astype(jnp.float32))).astype(o_ref.dtype)

    scores(0)
    for t in range(len(pairs)):
        if t + 1 < len(pairs):
            scores(t + 1)
        accumulate(t)


def _attention_a(proj, vt, lam_q1, lam_k1, lam_q2, lam_k2, subln, batch, seq, lam_init, tile=512):
    w2 = 2 * A_HEAD_DIM
    nh = proj.shape[0] // 3
    nt = seq // tile
    assert vt.shape[2] == tile
    vec = lambda a: a.reshape(1, -1)
    vec_spec = pl.BlockSpec((1, A_HEAD_DIM), lambda b, h: (0, 0))
    slab = lambda off: pl.BlockSpec((1, seq, w2), lambda b, h: (off + h, b, 0))
    return pl.pallas_call(
        functools.partial(_attn_a_kernel, tile=tile, n_heads=nh, lam_init=lam_init),
        grid=(batch, nh),
        in_specs=[slab(0), slab(nh),
                  pl.BlockSpec((nt, w2, tile), lambda b, h: (b, h, 0)),
                  slab(2 * nh),
                  vec_spec, vec_spec, vec_spec, vec_spec,
                  pl.BlockSpec((1, w2), lambda b, h: (0, 0))],
        out_specs=slab(0),
        out_shape=jax.ShapeDtypeStruct((nh, batch * seq, w2), jnp.bfloat16),
        scratch_shapes=[pltpu.VMEM((2, 2, tile, tile), jnp.float32),
                        pltpu.VMEM((nt, tile, A_HEAD_DIM), jnp.bfloat16),
                        pltpu.VMEM((tile // 2, tile // 2), jnp.float32),
                        pltpu.VMEM((nt, 2, w2, tile), jnp.float32)],
        compiler_params=_params("parallel", "arbitrary"),
        name="diff_attention",
    )(proj, proj, vt, proj, vec(lam_q1), vec(lam_k1), vec(lam_q2), vec(lam_k2), vec(subln))


_B_BATCH = 8


def _attn_b_kernel(q0_ref, k0_ref, v0_ref, q1_ref, k1_ref, v1_ref, q2_ref, k2_ref, v2_ref, z_ref,
                   o_ref, out_sc, lse_sc, *, seq, n_heads, groups):
    hd = B_HEAD_DIM
    h = pl.program_id(1)
    slope = _alibi_slope_log2(h, n_heads)
    qkv = ((q0_ref, k0_ref, v0_ref), (q1_ref, k1_ref, v1_ref), (q2_ref, k2_ref, v2_ref))

    stages = []
    for g, (window, dil) in enumerate(groups):
        w = window // dil
        n_per = (seq // dil) // w
        q_ref, k_ref, v_ref = qkv[g]

        def band_bias(n_keys, w=w, dil=dil):
            ii = lax.broadcasted_iota(jnp.int32, (w, n_keys), 0)
            jj = lax.broadcasted_iota(jnp.int32, (w, n_keys), 1)
            rel = ii + (n_keys - w) - jj
            return jnp.where((rel >= 0) & (rel <= w),
                             -slope * (rel * dil).astype(jnp.float32), NEG_BIG)

        biases = {w: band_bias(w)}
        if n_per > 1:
            biases[2 * w] = band_bias(2 * w)
        n_blocks = seq // w

        def score_stage(b0, w=w, n_per=n_per, n_blocks=n_blocks, biases=biases,
                        q_ref=q_ref, k_ref=k_ref):
            blocks = []
            for n in range(b0, min(b0 + _B_BATCH, n_blocks)):
                res, nb = divmod(n, n_per)
                n_keys = w if nb == 0 else 2 * w
                key_rows = pl.ds((n + 1) * w - n_keys, n_keys)
                q = q_ref[0, pl.ds(n * w, w), :]
                s = lax.dot_general(q, k_ref[0, key_rows, :], (((1,), (1,)), ((), ())),
                                    preferred_element_type=jnp.float32) + biases[n_keys]
                blocks.append((res, nb, key_rows, s))
            return blocks

        def value_stage(blocks, w=w, dil=dil, g=g, v_ref=v_ref):
            soft = []
            for res, nb, key_rows, s in blocks:
                m = jnp.max(s, axis=-1, keepdims=True)
                p = jnp.exp2(s - m)
                den = jnp.sum(p, axis=-1, keepdims=True)
                soft.append((p.astype(v_ref.dtype), den, m + jnp.log2(den)))
            for (res, nb, key_rows, _), (p, den, lse) in zip(blocks, soft):
                o = jnp.dot(p, v_ref[0, key_rows, :], preferred_element_type=jnp.float32) / den
                t0 = nb * (w * dil) + res
                dst = pl.ds(t0, w, stride=dil) if dil > 1 else pl.ds(t0, w)
                out_sc[g, dst, :] = o
                lse_sc[g, dst, :] = jnp.broadcast_to(lse, (w, hd))

        stages += [(score_stage, b0, value_stage) for b0 in range(0, n_blocks, _B_BATCH)]

    for score_stage, b0, value_stage in stages:
        value_stage(score_stage(b0))

    tc = 256

    def combine(c, carry):
        rows = pl.ds(pl.multiple_of(c * tc, tc), tc)
        lses = [lse_sc[g, rows, :] for g in range(len(groups))]
        top = functools.reduce(jnp.maximum, lses)
        wts = [jnp.exp2(l - top) for l in lses]
        num = functools.reduce(lambda a, b: a + b,
                               [wt * out_sc[g, rows, :] for g, wt in enumerate(wts)])
        o = num / functools.reduce(lambda a, b: a + b, wts)
        o_ref[0, rows, :] = (o * _silu(z_ref[0, rows, :].astype(jnp.float32))).astype(o_ref.dtype)
        return carry

    lax.fori_loop(0, seq // tc, combine, 0)


def _attention_b(proj_nat, proj_g1, proj_g2, batch, seq):
    hd = B_HEAD_DIM
    nh = proj_g1.shape[0] // 3
    assert all(seq % window == 0 for window, _ in B_GROUPS)
    slab = lambda off: pl.BlockSpec((1, seq, hd), lambda b, h: (off + h, b, 0))
    return pl.pallas_call(
        functools.partial(_attn_b_kernel, seq=seq, n_heads=nh, groups=B_GROUPS),
        grid=(batch, nh),
        in_specs=[slab(0), slab(nh), slab(2 * nh),
                  slab(0), slab(nh), slab(2 * nh),
                  slab(0), slab(nh), slab(2 * nh),
                  slab(3 * nh)],
        out_specs=pl.BlockSpec((1, seq, hd), lambda b, h: (h, b, 0)),
        out_shape=jax.ShapeDtypeStruct((nh, batch * seq, hd), jnp.bfloat16),
        scratch_shapes=[pltpu.VMEM((len(B_GROUPS), seq, hd), jnp.float32),
                        pltpu.VMEM((len(B_GROUPS), seq, hd), jnp.float32)],
        compiler_params=_params("parallel", "parallel"),
        name="dilated_attention",
    )(proj_nat, proj_nat, proj_nat, proj_g1, proj_g1, proj_g1, proj_g2, proj_g2, proj_g2, proj_nat)


def kernel(x, norm_a, w_in_a, lam_q1, lam_k1, lam_q2, lam_k2, subln_a, w_out_a,
           norm_b, w_in_b, w_out_b, norm_f):
    batch, seq, d = x.shape
    m = batch * seq
    bf16 = jnp.bfloat16
    x2 = x.reshape(m, d)
    tn = 1024
    a_tile = 512

    lam_init = 0.8 - 0.6 * math.exp(-0.3 * 0)
    q_scale_a = A_HEAD_DIM ** -0.5 * LOG2E
    hn = _rmsnorm(x2, norm_a[0])
    v_blk0, v_blk1 = 2 * d // tn, 3 * d // tn
    col_map_a = lambda j: jnp.where(j < v_blk0, j, j + (v_blk1 - v_blk0))
    proj_a = _project(hn, w_in_a[0], 3 * d, col_map_a, 2 * A_HEAD_DIM, "in_proj_a",
                      n_scaled=d // tn, col_scale=q_scale_a, tn=tn)
    vt_a = _project_transposed(hn, w_in_a[0], d, v_blk0, a_tile, "in_proj_a_vt", tn=tn)
    gated_a = _attention_a(proj_a, vt_a, lam_q1[0], lam_k1[0], lam_q2[0], lam_k2[0], subln_a[0],
                           batch, seq, lam_init, tile=a_tile)
    h1, hn1, hn4, hn16 = _out_project_a(gated_a, w_out_a[0].astype(bf16), x2, norm_b[0], batch, seq)

    q_scale_b = B_HEAD_DIM ** -0.5 * LOG2E
    w_in = w_in_b[0]
    group_cols = 3 * d
    n_grp = group_cols // tn
    gate_blk0 = len(B_GROUPS) * n_grp
    col_map_0 = lambda j: jnp.where(j < n_grp, j, j + (gate_blk0 - n_grp))
    proj_nat = _project(hn1, w_in, group_cols + d, col_map_0, B_HEAD_DIM, "in_proj_b0",
                        n_scaled=d // tn, col_scale=q_scale_b, tn=tn)
    proj_g1 = _project(hn4.reshape(m, d), w_in, group_cols, lambda j: n_grp + j, B_HEAD_DIM,
                       "in_proj_b1", n_scaled=d // tn, col_scale=q_scale_b, tn=tn)
    proj_g2 = _project(hn16.reshape(m, d), w_in, group_cols, lambda j: 2 * n_grp + j, B_HEAD_DIM,
                       "in_proj_b2", n_scaled=d // tn, col_scale=q_scale_b, tn=tn)
    gated_b = _attention_b(proj_nat, proj_g1, proj_g2, batch, seq)
    out = _out_project_b(gated_b, w_out_b[0].astype(bf16), h1, norm_f)
    return out.reshape(batch, seq, d)
```

```python
import functools
import math

import jax
import jax.numpy as jnp
from jax import lax
from jax.experimental import pallas as pl
from jax.experimental.pallas import tpu as pltpu

A_HEAD_DIM = 128
B_GROUPS = ((128, 1), (512, 4), (2048, 16))
B_HEAD_DIM = 128
RMS_EPS = 1e-6
NEG_BIG = -1e30
LOG2E = math.log2(math.e)
LANES = 128

VMEM_LIMIT_BYTES = 48 * 1024 * 1024


def _params(*semantics):
    return pltpu.CompilerParams(dimension_semantics=semantics, vmem_limit_bytes=VMEM_LIMIT_BYTES)


def _alibi_slope_log2(h, n_heads):
    return LOG2E * jnp.exp2(jnp.zeros((1, 1), jnp.float32)
                            - (8.0 / n_heads) * (h + 1).astype(jnp.float32))


def _silu(z):
    return z * jax.nn.sigmoid(z)


def _aligned(start, multiple):
    return start if isinstance(start, int) else pl.multiple_of(start, multiple)


def _proj_kernel(a_ref, w_ref, o_ref, w_sc, *, cb, n_scaled, col_scale):
    @pl.when(pl.program_id(1) == 0)
    def _():
        w_sc[...] = w_ref[...].astype(w_sc.dtype)

    acc = jnp.dot(a_ref[...], w_sc[...], preferred_element_type=jnp.float32)
    if n_scaled:
        acc = acc * jnp.where(pl.program_id(0) < n_scaled, col_scale, 1.0)
    for c in range(o_ref.shape[0]):
        o_ref[c] = acc[:, c * cb:(c + 1) * cb].astype(o_ref.dtype)


def _project(a, w, n_out, col_block_map, cb, name, n_scaled=0, col_scale=1.0, tm=1024, tn=1024):
    m, k = a.shape
    return pl.pallas_call(
        functools.partial(_proj_kernel, cb=cb, n_scaled=n_scaled, col_scale=col_scale),
        grid=(n_out // tn, m // tm),
        in_specs=[pl.BlockSpec((tm, k), lambda j, i: (i, 0)),
                  pl.BlockSpec((k, tn), lambda j, i: (0, col_block_map(j)))],
        out_specs=pl.BlockSpec((tn // cb, tm, cb), lambda j, i: (j, i, 0)),
        out_shape=jax.ShapeDtypeStruct((n_out // cb, m, cb), jnp.bfloat16),
        scratch_shapes=[pltpu.VMEM((k, tn), jnp.bfloat16)],
        compiler_params=_params("arbitrary", "arbitrary"),
        name=name,
    )(a, w)


def _norm_proj_t_kernel(x_ref, g_ref, w_ref, hn_ref, o_ref, wt_sc):
    @pl.when(pl.program_id(0) == 0)
    def _():
        wt_sc[...] = w_ref[...].T.astype(wt_sc.dtype)

    x = x_ref[...]
    ms = jnp.mean(x * x, axis=-1, keepdims=True)
    hn = (x * lax.rsqrt(ms + RMS_EPS) * g_ref[...]).astype(hn_ref.dtype)
    hn_ref[...] = hn
    o_ref[0] = lax.dot_general(wt_sc[...], hn, (((1,), (1,)), ((), ())),
                               preferred_element_type=jnp.float32).astype(o_ref.dtype)


def _norm_project_transposed(x, gain, w, n_out, col_block0, name, tm=512):
    m, k = x.shape
    const = lambda shape, index: pl.BlockSpec(shape, lambda i: index, pipeline_mode=pl.Buffered(1))
    return pl.pallas_call(
        _norm_proj_t_kernel,
        grid=(m // tm,),
        in_specs=[pl.BlockSpec((tm, k), lambda i: (i, 0)),
                  const((1, k), (0, 0)),
                  const((k, n_out), (0, col_block0))],
        out_specs=[pl.BlockSpec((tm, k), lambda i: (i, 0)),
                   pl.BlockSpec((1, n_out, tm), lambda i: (i, 0, 0))],
        out_shape=[jax.ShapeDtypeStruct((m, k), jnp.bfloat16),
                   jax.ShapeDtypeStruct((m // tm, n_out, tm), jnp.bfloat16)],
        scratch_shapes=[pltpu.VMEM((n_out, k), jnp.bfloat16)],
        compiler_params=_params("arbitrary"),
        name=name,
    )(x, gain.reshape(1, k), w)


def _gather_slabs(a_ref):
    return jnp.concatenate([a_ref[c] for c in range(a_ref.shape[0])], axis=-1)


def _out_proj_a_kernel(a_ref, w_ref, r_ref, g_ref, h_ref, hn1_ref, hn4_ref, hn16_ref,
                       slab_sc, slab4_sc):
    y = jnp.dot(_gather_slabs(a_ref), w_ref[...], preferred_element_type=jnp.float32) + r_ref[...]
    h_ref[...] = y
    ms = jnp.mean(y * y, axis=-1, keepdims=True)
    hn = y * lax.rsqrt(ms + RMS_EPS) * g_ref[...]
    hn1_ref[...] = hn.astype(hn1_ref.dtype)
    tm, d = hn.shape
    n_slab = d // LANES
    for c in range(n_slab):
        slab_sc[c] = hn[:, c * LANES:(c + 1) * LANES]
    dil, dil2 = hn4_ref.shape[1], hn16_ref.shape[1]
    step2 = dil2 // dil
    assert dil * step2 == dil2
    n4 = tm // dil
    for r in range(dil):
        pieces = [slab_sc[c, pl.ds(r, n4, stride=dil), :] for c in range(n_slab)]
        hn4_ref[0, r] = jnp.concatenate(pieces, axis=-1).astype(hn4_ref.dtype)
        for c in range(n_slab):
            slab4_sc[c, r * n4:(r + 1) * n4, :] = pieces[c]
    for r in range(dil2):
        rows = jnp.concatenate(
            [slab4_sc[c, pl.ds((r % dil) * n4 + r // dil, tm // dil2, stride=step2), :]
             for c in range(n_slab)], axis=-1)
        hn16_ref[0, r] = rows.astype(hn16_ref.dtype)


def _out_project_a(a, w, resid, gain, batch, seq, tm=256):
    nc, m, cb = a.shape
    k, n = w.shape
    nb = seq // tm
    const = lambda shape: pl.BlockSpec(shape, lambda i: (0,) * len(shape),
                                       pipeline_mode=pl.Buffered(1))
    perm_spec = lambda dil: pl.BlockSpec((1, dil, tm // dil, n), lambda i: (i // nb, 0, i % nb, 0))
    perm_shape = lambda dil: jax.ShapeDtypeStruct((batch, dil, seq // dil, n), jnp.bfloat16)
    return pl.pallas_call(
        _out_proj_a_kernel,
        grid=(m // tm,),
        in_specs=[pl.BlockSpec((nc, tm, cb), lambda i: (0, i, 0)),
                  const((k, n)),
                  pl.BlockSpec((tm, n), lambda i: (i, 0)),
                  const((1, n))],
        out_specs=[pl.BlockSpec((tm, n), lambda i: (i, 0)),
                   pl.BlockSpec((tm, n), lambda i: (i, 0)),
                   perm_spec(4), perm_spec(16)],
        out_shape=[jax.ShapeDtypeStruct((m, n), jnp.float32),
                   jax.ShapeDtypeStruct((m, n), jnp.bfloat16),
                   perm_shape(4), perm_shape(16)],
        scratch_shapes=[pltpu.VMEM((n // LANES, tm, LANES), jnp.float32)] * 2,
        compiler_params=_params("parallel"),
        name="out_proj_a",
    )(a, w, resid, gain.reshape(1, n))


def _out_proj_b_kernel(a_ref, w_ref, r_ref, g_ref, o_ref):
    y = jnp.dot(_gather_slabs(a_ref), w_ref[...], preferred_element_type=jnp.float32) + r_ref[...]
    ms = jnp.mean(y * y, axis=-1, keepdims=True)
    o_ref[...] = y * lax.rsqrt(ms + RMS_EPS) * g_ref[...]


def _out_project_b(a, w, resid, gain, tm=512):
    nc, m, cb = a.shape
    k, n = w.shape
    const = lambda shape: pl.BlockSpec(shape, lambda i: (0,) * len(shape),
                                       pipeline_mode=pl.Buffered(1))
    return pl.pallas_call(
        _out_proj_b_kernel,
        grid=(m // tm,),
        in_specs=[pl.BlockSpec((nc, tm, cb), lambda i: (0, i, 0)),
                  const((k, n)),
                  pl.BlockSpec((tm, n), lambda i: (i, 0)),
                  const((1, n))],
        out_specs=pl.BlockSpec((tm, n), lambda i: (i, 0)),
        out_shape=jax.ShapeDtypeStruct((m, n), jnp.float32),
        compiler_params=_params("parallel"),
        name="out_proj_b",
    )(a, w, resid, gain.reshape(1, n))


_SLOPE_PIECES = 3
_POS_RADIX = 64

def _attn_a_kernel(q_ref, k_ref, vt_ref, z_ref, lq1_ref, lk1_ref, lq2_ref, lk2_ref, sub_ref,
                   o_ref, s_sc, kfeat_sc, mask_sc, *, tile, n_heads, lam_init):
    hd = A_HEAD_DIM
    seq = q_ref.shape[1]
    nt = seq // tile
    h = pl.program_id(1)
    slope = _alibi_slope_log2(h, n_heads)
    lam = (jnp.exp(jnp.sum(lq1_ref[...] * lk1_ref[...], axis=-1, keepdims=True))
           - jnp.exp(jnp.sum(lq2_ref[...] * lk2_ref[...], axis=-1, keepdims=True)) + lam_init)

    bf16 = k_ref.dtype
    col = lax.broadcasted_iota(jnp.int32, (tile, hd), 1)
    row = lax.broadcasted_iota(jnp.int32, (tile, hd), 0)
    pieces, rest = [], slope
    for _ in range(_SLOPE_PIECES):
        piece = rest.astype(bf16).astype(jnp.float32)
        pieces.append(piece)
        rest = rest - piece
    q_feat = jnp.zeros((tile, hd), jnp.float32)
    for n, piece in enumerate(pieces):
        q_feat = jnp.where(col == n, _POS_RADIX * piece, q_feat)
        q_feat = jnp.where(col == _SLOPE_PIECES + n, piece, q_feat)
    q_feat = q_feat.astype(bf16)
    for kt in range(nt):
        pos = row + kt * tile
        hi = (pos // _POS_RADIX).astype(jnp.float32)
        lo = (pos % _POS_RADIX).astype(jnp.float32)
        kfeat_sc[kt] = jnp.where(col < _SLOPE_PIECES, hi,
                                 jnp.where(col < 2 * _SLOPE_PIECES, lo, 0.0)).astype(bf16)
    half = tile // 2
    mask_sc[...] = jnp.where(lax.broadcasted_iota(jnp.int32, (half, half), 1)
                             >= lax.broadcasted_iota(jnp.int32, (half, half), 0), 0.0, -jnp.inf)

    def panels(qt):
        k0 = qt * tile
        return ([((kt * tile, (kt + 1) * tile), (0, tile), None) for kt in range(qt)]
                + [((k0, k0 + half), (0, half), 0), ((k0, k0 + tile), (half, tile), half)])

    def scores(qt):
        q = q_ref[0, pl.ds(qt * tile, tile), :]
        for i in range(2):
            q_aug = jnp.concatenate([q[:, i * hd:(i + 1) * hd], q_feat], axis=1)
            maxima = [None, None]
            for (k0, k1), (c0, c1), mask_row in panels(qt):
                k_aug = jnp.concatenate([k_ref[0, k0:k1, i * hd:(i + 1) * hd],
                                         kfeat_sc[k0 // tile, k0 % tile:k0 % tile + k1 - k0, :]], axis=1)
                s = lax.dot_general(k_aug, q_aug[c0:c1], (((1,), (1,)), ((), ())),
                                    preferred_element_type=jnp.float32)
                if mask_row == 0:
                    s = s + mask_sc[...]
                elif mask_row is not None:
                    s = jnp.concatenate([s[:mask_row], s[mask_row:] + mask_sc[...]], axis=0)
                s_sc[qt % 2, i, k0:k1, c0:c1] = s
                top = jnp.max(s, axis=0, keepdims=True)
                for hf in range(2):
                    lo, hi = max(c0, hf * half), min(c1, (hf + 1) * half)
                    if lo < hi:
                        piece = top[:, lo - c0:hi - c0]
                        maxima[hf] = piece if maxima[hf] is None else jnp.maximum(maxima[hf], piece)
            col_max[qt, i] = jnp.concatenate(maxima, axis=1)

    col_max = {}
    sub_gain = sub_ref[...] * (1.0 - lam_init)

    def finish(qt):
        heads = []
        for i in range(2):
            top = col_max.pop((qt, i))
            sums = [None, None]
            out = [None, None]
            for (k0, k1), (c0, c1), _ in panels(qt):
                p = jnp.exp2(s_sc[qt % 2, i, k0:k1, c0:c1] - top[:, c0:c1])
                part = jnp.sum(p, axis=0, keepdims=True)
                vtc = vt_ref[k0 // tile][:, k0 % tile:k0 % tile + k1 - k0]
                pv = jnp.dot(vtc, p.astype(vtc.dtype), preferred_element_type=jnp.float32)
                for hf in range(2):
                    lo, hi = max(c0, hf * half), min(c1, (hf + 1) * half)
                    if lo < hi:
                        add = lambda old, new: new if old is None else old + new
                        sums[hf] = add(sums[hf], part[:, lo - c0:hi - c0])
                        out[hf] = add(out[hf], pv[:, lo - c0:hi - c0])
            heads.append(jnp.concatenate(out, axis=1) / jnp.concatenate(sums, axis=1))
        rows = pl.ds(qt * tile, tile)
        o = (heads[0] - lam * heads[1]).T
        ms = jnp.mean(o * o, axis=-1, keepdims=True)
        o = o * lax.rsqrt(ms + RMS_EPS) * sub_gain
        o_ref[0, rows, :] = (o * _silu(z_ref[0, rows, :].astype(jnp.float32))).astype(o_ref.dtype)

    scores(0)
    for qt in range(nt):
        if qt + 1 < nt:
            scores(qt + 1)
        finish(qt)


def _attention_a(proj, vt, lam_q1, lam_k1, lam_q2, lam_k2, subln, batch, seq, lam_init, tile=512):
    w2 = 2 * A_HEAD_DIM
    nh = proj.shape[0] // 3
    nt = seq // tile
    assert vt.shape[2] == tile
    vec = lambda a: a.reshape(1, -1)
    vec_spec = pl.BlockSpec((1, A_HEAD_DIM), lambda b, h: (0, 0))
    slab = lambda off: pl.BlockSpec((1, seq, w2), lambda b, h: (off + h, b, 0))
    return pl.pallas_call(
        functools.partial(_attn_a_kernel, tile=tile, n_heads=nh, lam_init=lam_init),
        grid=(batch, nh),
        in_specs=[slab(0), slab(nh),
                  pl.BlockSpec((nt, w2, tile), lambda b, h: (b, h, 0)),
                  slab(2 * nh),
                  vec_spec, vec_spec, vec_spec, vec_spec,
                  pl.BlockSpec((1, w2), lambda b, h: (0, 0))],
        out_specs=slab(0),
        out_shape=jax.ShapeDtypeStruct((nh, batch * seq, w2), jnp.bfloat16),
        scratch_shapes=[pltpu.VMEM((2, 2, seq, tile), jnp.float32),
                        pltpu.VMEM((nt, tile, A_HEAD_DIM), jnp.bfloat16),
                        pltpu.VMEM((tile // 2, tile // 2), jnp.float32)],
        compiler_params=_params("parallel", "arbitrary"),
        name="diff_attention",
    )(proj, proj, vt, proj, vec(lam_q1), vec(lam_k1), vec(lam_q2), vec(lam_k2), vec(subln))


_B_BATCH = 8


def _attn_b_kernel(q0_ref, k0_ref, v0_ref, q1_ref, k1_ref, v1_ref, q2_ref, k2_ref, v2_ref, z_ref,
                   o_ref, out_sc, lse_sc, *, seq, n_heads, groups):
    hd = B_HEAD_DIM
    h = pl.program_id(1)
    slope = _alibi_slope_log2(h, n_heads)
    qkv = ((q0_ref, k0_ref, v0_ref), (q1_ref, k1_ref, v1_ref), (q2_ref, k2_ref, v2_ref))

    stages = []
    for g, (window, dil) in enumerate(groups):
        w = window // dil
        n_per = (seq // dil) // w
        q_ref, k_ref, v_ref = qkv[g]

        def band_bias(n_keys, w=w, dil=dil):
            ii = lax.broadcasted_iota(jnp.int32, (w, n_keys), 0)
            jj = lax.broadcasted_iota(jnp.int32, (w, n_keys), 1)
            rel = ii + (n_keys - w) - jj
            return jnp.where((rel >= 0) & (rel <= w),
                             -slope * (rel * dil).astype(jnp.float32), NEG_BIG)

        biases = {w: band_bias(w)}
        if n_per > 1:
            biases[2 * w] = band_bias(2 * w)
        n_blocks = seq // w

        def score_stage(b0, w=w, n_per=n_per, n_blocks=n_blocks, biases=biases,
                        q_ref=q_ref, k_ref=k_ref):
            blocks = []
            for n in range(b0, min(b0 + _B_BATCH, n_blocks)):
                res, nb = divmod(n, n_per)
                n_keys = w if nb == 0 else 2 * w
                key_rows = pl.ds((n + 1) * w - n_keys, n_keys)
                q = q_ref[0, pl.ds(n * w, w), :]
                s = lax.dot_general(q, k_ref[0, key_rows, :], (((1,), (1,)), ((), ())),
                                    preferred_element_type=jnp.float32) + biases[n_keys]
                blocks.append((res, nb, key_rows, s))
            return blocks

        def value_stage(blocks, w=w, dil=dil, g=g, v_ref=v_ref):
            soft = []
            for res, nb, key_rows, s in blocks:
                m = jnp.max(s, axis=-1, keepdims=True)
                p = jnp.exp2(s - m)
                den = jnp.sum(p, axis=-1, keepdims=True)
                soft.append((p.astype(v_ref.dtype), den, m + jnp.log2(den)))
            for (res, nb, key_rows, _), (p, den, lse) in zip(blocks, soft):
                o = jnp.dot(p, v_ref[0, key_rows, :], preferred_element_type=jnp.float32) / den
                t0 = nb * (w * dil) + res
                dst = pl.ds(t0, w, stride=dil) if dil > 1 else pl.ds(t0, w)
                out_sc[g, dst, :] = o
                lse_sc[g, dst, :] = jnp.broadcast_to(lse, (w, hd))

        stages += [(score_stage, b0, value_stage) for b0 in range(0, n_blocks, _B_BATCH)]

    for score_stage, b0, value_stage in stages:
        value_stage(score_stage(b0))

    tc = 256

    def combine(c, carry):
        rows = pl.ds(pl.multiple_of(c * tc, tc), tc)
        lses = [lse_sc[g, rows, :] for g in range(len(groups))]
        top = functools.reduce(jnp.maximum, lses)
        wts = [jnp.exp2(l - top) for l in lses]
        num = functools.reduce(lambda a, b: a + b,
                               [wt * out_sc[g, rows, :] for g, wt in enumerate(wts)])
        o = num / functools.reduce(lambda a, b: a + b, wts)
        o_ref[0, rows, :] = (o * _silu(z_ref[0, rows, :].astype(jnp.float32))).astype(o_ref.dtype)
        return carry

    lax.fori_loop(0, seq // tc, combine, 0)


def _attention_b(proj_nat, proj_g1, proj_g2, batch, seq):
    hd = B_HEAD_DIM
    nh = proj_g1.shape[0] // 3
    assert all(seq % window == 0 for window, _ in B_GROUPS)
    slab = lambda off: pl.BlockSpec((1, seq, hd), lambda b, h: (off + h, b, 0))
    return pl.pallas_call(
        functools.partial(_attn_b_kernel, seq=seq, n_heads=nh, groups=B_GROUPS),
        grid=(batch, nh),
        in_specs=[slab(0), slab(nh), slab(2 * nh),
                  slab(0), slab(nh), slab(2 * nh),
                  slab(0), slab(nh), slab(2 * nh),
                  slab(3 * nh)],
        out_specs=pl.BlockSpec((1, seq, hd), lambda b, h: (h, b, 0)),
        out_shape=jax.ShapeDtypeStruct((nh, batch * seq, hd), jnp.bfloat16),
        scratch_shapes=[pltpu.VMEM((len(B_GROUPS), seq, hd), jnp.float32),
                        pltpu.VMEM((len(B_GROUPS), seq, hd), jnp.float32)],
        compiler_params=_params("parallel", "parallel"),
        name="dilated_attention",
    )(proj_nat, proj_nat, proj_nat, proj_g1, proj_g1, proj_g1, proj_g2, proj_g2, proj_g2, proj_nat)


def kernel(x, norm_a, w_in_a, lam_q1, lam_k1, lam_q2, lam_k2, subln_a, w_out_a,
           norm_b, w_in_b, w_out_b, norm_f):
    batch, seq, d = x.shape
    m = batch * seq
    bf16 = jnp.bfloat16
    x2 = x.reshape(m, d)
    tn = 1024
    a_tile = 512

    lam_init = 0.8 - 0.6 * math.exp(-0.3 * 0)
    q_scale_a = A_HEAD_DIM ** -0.5 * LOG2E
    hn, vt_a = _norm_project_transposed(x2, norm_a[0], w_in_a[0], d, 2, "norm_in_proj_a_vt",
                                        tm=a_tile)
    v_blk0, v_blk1 = 2 * d // tn, 3 * d // tn
    col_map_a = lambda j: jnp.where(j < v_blk0, j, j + (v_blk1 - v_blk0))
    proj_a = _project(hn, w_in_a[0], 3 * d, col_map_a, 2 * A_HEAD_DIM, "in_proj_a",
                      n_scaled=d // tn, col_scale=q_scale_a, tn=tn)
    gated_a = _attention_a(proj_a, vt_a, lam_q1[0], lam_k1[0], lam_q2[0], lam_k2[0], subln_a[0],
                           batch, seq, lam_init, tile=a_tile)
    h1, hn1, hn4, hn16 = _out_project_a(gated_a, w_out_a[0].astype(bf16), x2, norm_b[0], batch, seq)

    q_scale_b = B_HEAD_DIM ** -0.5 * LOG2E
    w_in = w_in_b[0]
    group_cols = 3 * d
    n_grp = group_cols // tn
    gate_blk0 = len(B_GROUPS) * n_grp
    col_map_0 = lambda j: jnp.where(j < n_grp, j, j + (gate_blk0 - n_grp))
    proj_nat = _project(hn1, w_in, group_cols + d, col_map_0, B_HEAD_DIM, "in_proj_b0",
                        n_scaled=d // tn, col_scale=q_scale_b, tn=tn)
    proj_g1 = _project(hn4.reshape(m, d), w_in, group_cols, lambda j: n_grp + j, B_HEAD_DIM,
                       "in_proj_b1", n_scaled=d // tn, col_scale=q_scale_b, tn=tn)
    proj_g2 = _project(hn16.reshape(m, d), w_in, group_cols, lambda j: 2 * n_grp + j, B_HEAD_DIM,
                       "in_proj_b2", n_scaled=d // tn, col_scale=q_scale_b, tn=tn)
    gated_b = _attention_b(proj_nat, proj_g1, proj_g2, batch, seq)
    out = _out_project_b(gated_b, w_out_b[0].astype(bf16), h1, norm_f)
    return out.reshape(batch, seq, d)
```

```python
import functools
import math

import jax
import jax.numpy as jnp
from jax import lax
from jax.experimental import pallas as pl
from jax.experimental.pallas import tpu as pltpu

A_HEAD_DIM = 128
B_GROUPS = ((128, 1), (512, 4), (2048, 16))
B_HEAD_DIM = 128
RMS_EPS = 1e-6
NEG_BIG = -1e30
LOG2E = math.log2(math.e)
LANES = 128

VMEM_LIMIT_BYTES = 48 * 1024 * 1024


def _params(*semantics):
    return pltpu.CompilerParams(dimension_semantics=semantics, vmem_limit_bytes=VMEM_LIMIT_BYTES)


def _alibi_slope_log2(h, n_heads):
    return LOG2E * jnp.exp2(jnp.zeros((1, 1), jnp.float32)
                            - (8.0 / n_heads) * (h + 1).astype(jnp.float32))


def _silu(z):
    return z * jax.nn.sigmoid(z)


def _aligned(start, multiple):
    return start if isinstance(start, int) else pl.multiple_of(start, multiple)


def _proj_kernel(a_ref, w_ref, o_ref, w_sc, *, cb, n_scaled, col_scale):
    @pl.when(pl.program_id(1) == 0)
    def _():
        w_sc[...] = w_ref[...].astype(w_sc.dtype)

    acc = jnp.dot(a_ref[...], w_sc[...], preferred_element_type=jnp.float32)
    if n_scaled:
        acc = acc * jnp.where(pl.program_id(0) < n_scaled, col_scale, 1.0)
    for c in range(o_ref.shape[0]):
        o_ref[c] = acc[:, c * cb:(c + 1) * cb].astype(o_ref.dtype)


def _project(a, w, n_out, col_block_map, cb, name, n_scaled=0, col_scale=1.0, tm=1024, tn=1024):
    m, k = a.shape
    return pl.pallas_call(
        functools.partial(_proj_kernel, cb=cb, n_scaled=n_scaled, col_scale=col_scale),
        grid=(n_out // tn, m // tm),
        in_specs=[pl.BlockSpec((tm, k), lambda j, i: (i, 0)),
                  pl.BlockSpec((k, tn), lambda j, i: (0, col_block_map(j)))],
        out_specs=pl.BlockSpec((tn // cb, tm, cb), lambda j, i: (j, i, 0)),
        out_shape=jax.ShapeDtypeStruct((n_out // cb, m, cb), jnp.bfloat16),
        scratch_shapes=[pltpu.VMEM((k, tn), jnp.bfloat16)],
        compiler_params=_params("arbitrary", "arbitrary"),
        name=name,
    )(a, w)


def _norm_proj_t_kernel(x_ref, g_ref, w_ref, hn_ref, o_ref, wt_sc):
    @pl.when(pl.program_id(0) == 0)
    def _():
        wt_sc[...] = w_ref[...].T.astype(wt_sc.dtype)

    x = x_ref[...]
    ms = jnp.mean(x * x, axis=-1, keepdims=True)
    hn = (x * lax.rsqrt(ms + RMS_EPS) * g_ref[...]).astype(hn_ref.dtype)
    hn_ref[...] = hn
    o_ref[0] = lax.dot_general(wt_sc[...], hn, (((1,), (1,)), ((), ())),
                               preferred_element_type=jnp.float32).astype(o_ref.dtype)


def _norm_project_transposed(x, gain, w, n_out, col_block0, name, tm=512):
    m, k = x.shape
    const = lambda shape, index: pl.BlockSpec(shape, lambda i: index, pipeline_mode=pl.Buffered(1))
    return pl.pallas_call(
        _norm_proj_t_kernel,
        grid=(m // tm,),
        in_specs=[pl.BlockSpec((tm, k), lambda i: (i, 0)),
                  const((1, k), (0, 0)),
                  const((k, n_out), (0, col_block0))],
        out_specs=[pl.BlockSpec((tm, k), lambda i: (i, 0)),
                   pl.BlockSpec((1, n_out, tm), lambda i: (i, 0, 0))],
        out_shape=[jax.ShapeDtypeStruct((m, k), jnp.bfloat16),
                   jax.ShapeDtypeStruct((m // tm, n_out, tm), jnp.bfloat16)],
        scratch_shapes=[pltpu.VMEM((n_out, k), jnp.bfloat16)],
        compiler_params=_params("arbitrary"),
        name=name,
    )(x, gain.reshape(1, k), w)


def _proj_t_kernel(a_ref, w_ref, o_ref, wt_sc):
    @pl.when(pl.program_id(1) == 0)
    def _():
        wt_sc[...] = w_ref[...].T.astype(wt_sc.dtype)

    o_ref[0] = lax.dot_general(wt_sc[...], a_ref[...], (((1,), (1,)), ((), ())),
                               preferred_element_type=jnp.float32).astype(o_ref.dtype)


def _project_transposed(a, w, n_out, col_block0, name, tm, tn=512):
    m, k = a.shape
    return pl.pallas_call(
        _proj_t_kernel,
        grid=(n_out // tn, m // tm),
        in_specs=[pl.BlockSpec((tm, k), lambda j, i: (i, 0)),
                  pl.BlockSpec((k, tn), lambda j, i: (0, col_block0 + j))],
        out_specs=pl.BlockSpec((1, tn, tm), lambda j, i: (i, j, 0)),
        out_shape=jax.ShapeDtypeStruct((m // tm, n_out, tm), jnp.bfloat16),
        scratch_shapes=[pltpu.VMEM((tn, k), jnp.bfloat16)],
        compiler_params=_params("arbitrary", "arbitrary"),
        name=name,
    )(a, w)


def _gather_slabs(a_ref):
    return jnp.concatenate([a_ref[c] for c in range(a_ref.shape[0])], axis=-1)


def _out_proj_a_kernel(a_ref, w_ref, r_ref, g_ref, h_ref, hn1_ref, hn4_ref, hn16_ref,
                       slab_sc, slab4_sc):
    y = jnp.dot(_gather_slabs(a_ref), w_ref[...], preferred_element_type=jnp.float32) + r_ref[...]
    h_ref[...] = y
    ms = jnp.mean(y * y, axis=-1, keepdims=True)
    hn = y * lax.rsqrt(ms + RMS_EPS) * g_ref[...]
    hn1_ref[...] = hn.astype(hn1_ref.dtype)
    tm, d = hn.shape
    n_slab = d // LANES
    for c in range(n_slab):
        slab_sc[c] = hn[:, c * LANES:(c + 1) * LANES]
    dil, dil2 = hn4_ref.shape[1], hn16_ref.shape[1]
    step2 = dil2 // dil
    assert dil * step2 == dil2
    n4 = tm // dil
    for r in range(dil):
        pieces = [slab_sc[c, pl.ds(r, n4, stride=dil), :] for c in range(n_slab)]
        hn4_ref[0, r] = jnp.concatenate(pieces, axis=-1).astype(hn4_ref.dtype)
        for c in range(n_slab):
            slab4_sc[c, r * n4:(r + 1) * n4, :] = pieces[c]
    for r in range(dil2):
        rows = jnp.concatenate(
            [slab4_sc[c, pl.ds((r % dil) * n4 + r // dil, tm // dil2, stride=step2), :]
             for c in range(n_slab)], axis=-1)
        hn16_ref[0, r] = rows.astype(hn16_ref.dtype)


def _out_project_a(a, w, resid, gain, batch, seq, tm=256):
    nc, m, cb = a.shape
    k, n = w.shape
    nb = seq // tm
    const = lambda shape: pl.BlockSpec(shape, lambda i: (0,) * len(shape),
                                       pipeline_mode=pl.Buffered(1))
    perm_spec = lambda dil: pl.BlockSpec((1, dil, tm // dil, n), lambda i: (i // nb, 0, i % nb, 0))
    perm_shape = lambda dil: jax.ShapeDtypeStruct((batch, dil, seq // dil, n), jnp.bfloat16)
    return pl.pallas_call(
        _out_proj_a_kernel,
        grid=(m // tm,),
        in_specs=[pl.BlockSpec((nc, tm, cb), lambda i: (0, i, 0)),
                  const((k, n)),
                  pl.BlockSpec((tm, n), lambda i: (i, 0)),
                  const((1, n))],
        out_specs=[pl.BlockSpec((tm, n), lambda i: (i, 0)),
                   pl.BlockSpec((tm, n), lambda i: (i, 0)),
                   perm_spec(4), perm_spec(16)],
        out_shape=[jax.ShapeDtypeStruct((m, n), jnp.float32),
                   jax.ShapeDtypeStruct((m, n), jnp.bfloat16),
                   perm_shape(4), perm_shape(16)],
        scratch_shapes=[pltpu.VMEM((n // LANES, tm, LANES), jnp.float32)] * 2,
        compiler_params=_params("parallel"),
        name="out_proj_a",
    )(a, w, resid, gain.reshape(1, n))


def _out_proj_b_kernel(a_ref, w_ref, r_ref, g_ref, o_ref):
    y = jnp.dot(_gather_slabs(a_ref), w_ref[...], preferred_element_type=jnp.float32) + r_ref[...]
    ms = jnp.mean(y * y, axis=-1, keepdims=True)
    o_ref[...] = y * lax.rsqrt(ms + RMS_EPS) * g_ref[...]


def _out_project_b(a, w, resid, gain, tm=512):
    nc, m, cb = a.shape
    k, n = w.shape
    const = lambda shape: pl.BlockSpec(shape, lambda i: (0,) * len(shape),
                                       pipeline_mode=pl.Buffered(1))
    return pl.pallas_call(
        _out_proj_b_kernel,
        grid=(m // tm,),
        in_specs=[pl.BlockSpec((nc, tm, cb), lambda i: (0, i, 0)),
                  const((k, n)),
                  pl.BlockSpec((tm, n), lambda i: (i, 0)),
                  const((1, n))],
        out_specs=pl.BlockSpec((tm, n), lambda i: (i, 0)),
        out_shape=jax.ShapeDtypeStruct((m, n), jnp.float32),
        compiler_params=_params("parallel"),
        name="out_proj_b",
    )(a, w, resid, gain.reshape(1, n))


_SLOPE_PIECES = 3
_POS_RADIX = 64

def _attn_a_kernel(q_ref, k_ref, vt_ref, z_ref, lq1_ref, lk1_ref, lq2_ref, lk2_ref, sub_ref,
                   o_ref, s_sc, kfeat_sc, mask_sc, *, tile, n_heads, lam_init):
    hd = A_HEAD_DIM
    seq = q_ref.shape[1]
    nt = seq // tile
    h = pl.program_id(1)
    slope = _alibi_slope_log2(h, n_heads)
    lam = (jnp.exp(jnp.sum(lq1_ref[...] * lk1_ref[...], axis=-1, keepdims=True))
           - jnp.exp(jnp.sum(lq2_ref[...] * lk2_ref[...], axis=-1, keepdims=True)) + lam_init)

    bf16 = k_ref.dtype
    col = lax.broadcasted_iota(jnp.int32, (tile, hd), 1)
    row = lax.broadcasted_iota(jnp.int32, (tile, hd), 0)
    pieces, rest = [], slope
    for _ in range(_SLOPE_PIECES):
        piece = rest.astype(bf16).astype(jnp.float32)
        pieces.append(piece)
        rest = rest - piece
    q_feat = jnp.zeros((tile, hd), jnp.float32)
    for n, piece in enumerate(pieces):
        q_feat = jnp.where(col == n, _POS_RADIX * piece, q_feat)
        q_feat = jnp.where(col == _SLOPE_PIECES + n, piece, q_feat)
    q_feat = q_feat.astype(bf16)
    for kt in range(nt):
        pos = row + kt * tile
        hi = (pos // _POS_RADIX).astype(jnp.float32)
        lo = (pos % _POS_RADIX).astype(jnp.float32)
        kfeat_sc[kt] = jnp.where(col < _SLOPE_PIECES, hi,
                                 jnp.where(col < 2 * _SLOPE_PIECES, lo, 0.0)).astype(bf16)
    half = tile // 2
    mask_sc[...] = jnp.where(lax.broadcasted_iota(jnp.int32, (half, half), 1)
                             >= lax.broadcasted_iota(jnp.int32, (half, half), 0), 0.0, -jnp.inf)

    def panels(qt):
        k0 = qt * tile
        return ([((kt * tile, (kt + 1) * tile), (0, tile), None) for kt in range(qt)]
                + [((k0, k0 + half), (0, half), 0), ((k0, k0 + tile), (half, tile), half)])

    def scores(qt):
        q = q_ref[0, pl.ds(qt * tile, tile), :]
        for i in range(2):
            q_aug = jnp.concatenate([q[:, i * hd:(i + 1) * hd], q_feat], axis=1)
            maxima = [None, None]
            for (k0, k1), (c0, c1), mask_row in panels(qt):
                k_aug = jnp.concatenate([k_ref[0, k0:k1, i * hd:(i + 1) * hd],
                                         kfeat_sc[k0 // tile, k0 % tile:k0 % tile + k1 - k0, :]], axis=1)
                s = lax.dot_general(k_aug, q_aug[c0:c1], (((1,), (1,)), ((), ())),
                                    preferred_element_type=jnp.float32)
                if mask_row == 0:
                    s = s + mask_sc[...]
                elif mask_row is not None:
                    s = jnp.concatenate([s[:mask_row], s[mask_row:] + mask_sc[...]], axis=0)
                s_sc[qt % 2, i, k0:k1, c0:c1] = s
                top = jnp.max(s, axis=0, keepdims=True)
                for hf in range(2):
                    lo, hi = max(c0, hf * half), min(c1, (hf + 1) * half)
                    if lo < hi:
                        piece = top[:, lo - c0:hi - c0]
                        maxima[hf] = piece if maxima[hf] is None else jnp.maximum(maxima[hf], piece)
            col_max[qt, i] = jnp.concatenate(maxima, axis=1)

    col_max = {}
    sub_gain = sub_ref[...] * (1.0 - lam_init)

    def finish(qt):
        heads = []
        for i in range(2):
            top = col_max.pop((qt, i))
            sums = [None, None]
            out = [None, None]
            for (k0, k1), (c0, c1), _ in panels(qt):
                p = jnp.exp2(s_sc[qt % 2, i, k0:k1, c0:c1] - top[:, c0:c1])
                part = jnp.sum(p, axis=0, keepdims=True)
                vtc = vt_ref[k0 // tile][:, k0 % tile:k0 % tile + k1 - k0]
                pv = jnp.dot(vtc, p.astype(vtc.dtype), preferred_element_type=jnp.float32)
                for hf in range(2):
                    lo, hi = max(c0, hf * half), min(c1, (hf + 1) * half)
                    if lo < hi:
                        add = lambda old, new: new if old is None else old + new
                        sums[hf] = add(sums[hf], part[:, lo - c0:hi - c0])
                        out[hf] = add(out[hf], pv[:, lo - c0:hi - c0])
            heads.append(jnp.concatenate(out, axis=1) / jnp.concatenate(sums, axis=1))
        rows = pl.ds(qt * tile, tile)
        o = (heads[0] - lam * heads[1]).T
        ms = jnp.mean(o * o, axis=-1, keepdims=True)
        o = o * lax.rsqrt(ms + RMS_EPS) * sub_gain
        o_ref[0, rows, :] = (o * _silu(z_ref[0, rows, :].astype(jnp.float32))).astype(o_ref.dtype)

    scores(0)
    for qt in range(nt):
        if qt + 1 < nt:
            scores(qt + 1)
        finish(qt)


def _attention_a(proj, vt, lam_q1, lam_k1, lam_q2, lam_k2, subln, batch, seq, lam_init, tile=512):
    w2 = 2 * A_HEAD_DIM
    nh = proj.shape[0] // 3
    nt = seq // tile
    assert vt.shape[2] == tile
    vec = lambda a: a.reshape(1, -1)
    vec_spec = pl.BlockSpec((1, A_HEAD_DIM), lambda b, h: (0, 0))
    slab = lambda off: pl.BlockSpec((1, seq, w2), lambda b, h: (off + h, b, 0))
    return pl.pallas_call(
        functools.partial(_attn_a_kernel, tile=tile, n_heads=nh, lam_init=lam_init),
        grid=(batch, nh),
        in_specs=[slab(0), slab(nh),
                  pl.BlockSpec((nt, w2, tile), lambda b, h: (b, h, 0)),
                  slab(2 * nh),
                  vec_spec, vec_spec, vec_spec, vec_spec,
                  pl.BlockSpec((1, w2), lambda b, h: (0, 0))],
        out_specs=slab(0),
        out_shape=jax.ShapeDtypeStruct((nh, batch * seq, w2), jnp.bfloat16),
        scratch_shapes=[pltpu.VMEM((2, 2, seq, tile), jnp.float32),
                        pltpu.VMEM((nt, tile, A_HEAD_DIM), jnp.bfloat16),
                        pltpu.VMEM((tile // 2, tile // 2), jnp.float32)],
        compiler_params=_params("parallel", "arbitrary"),
        name="diff_attention",
    )(proj, proj, vt, proj, vec(lam_q1), vec(lam_k1), vec(lam_q2), vec(lam_k2), vec(subln))


_B_BATCH = 8


def _attn_b_kernel(q0_ref, k0_ref, v0_ref, q1_ref, k1_ref, v1_ref, q2_ref, k2_ref, v2_ref, z_ref,
                   o_ref, out_sc, lse_sc, *, seq, n_heads, groups):
    hd = B_HEAD_DIM
    h = pl.program_id(1)
    slope = _alibi_slope_log2(h, n_heads)
    qkv = ((q0_ref, k0_ref, v0_ref), (q1_ref, k1_ref, v1_ref), (q2_ref, k2_ref, v2_ref))

    stages = []
    for g, (window, dil) in enumerate(groups):
        w = window // dil
        n_per = (seq // dil) // w
        q_ref, k_ref, v_ref = qkv[g]

        def band_bias(n_keys, w=w, dil=dil):
            jj = lax.broadcasted_iota(jnp.int32, (n_keys, w), 0)
            ii = lax.broadcasted_iota(jnp.int32, (n_keys, w), 1)
            rel = ii + (n_keys - w) - jj
            return jnp.where((rel >= 0) & (rel <= w),
                             -slope * (rel * dil).astype(jnp.float32), NEG_BIG)

        biases = {w: band_bias(w)}
        if n_per > 1:
            biases[2 * w] = band_bias(2 * w)
        n_blocks = seq // w

        def score_stage(b0, w=w, n_per=n_per, n_blocks=n_blocks, biases=biases,
                        q_ref=q_ref, k_ref=k_ref):
            blocks = []
            for n in range(b0, min(b0 + _B_BATCH, n_blocks)):
                res, nb = divmod(n, n_per)
                n_keys = w if nb == 0 else 2 * w
                key_rows = pl.ds((n + 1) * w - n_keys, n_keys)
                q = q_ref[0, pl.ds(n * w, w), :]
                s = lax.dot_general(k_ref[0, key_rows, :], q, (((1,), (1,)), ((), ())),
                                    preferred_element_type=jnp.float32) + biases[n_keys]
                blocks.append((res, nb, key_rows, s))
            return blocks

        def value_stage(blocks, w=w, dil=dil, g=g, v_ref=v_ref):
            soft = []
            for res, nb, key_rows, s in blocks:
                m = jnp.max(s, axis=0, keepdims=True)
                p = jnp.exp2(s - m)
                den = jnp.sum(p, axis=0, keepdims=True)
                soft.append((p.astype(v_ref.dtype), den, m + jnp.log2(den)))
            for (res, nb, key_rows, _), (p, den, lse) in zip(blocks, soft):
                o_t = jnp.dot(v_ref[0, :, key_rows], p, preferred_element_type=jnp.float32) / den
                t0 = nb * (w * dil) + res
                dst = pl.ds(t0, w, stride=dil) if dil > 1 else pl.ds(t0, w)
                out_sc[g, dst, :] = o_t.T
                lse_sc[g, dst, :] = jnp.broadcast_to(lse, (hd, w)).T

        stages += [(score_stage, b0, value_stage) for b0 in range(0, n_blocks, _B_BATCH)]

    for score_stage, b0, value_stage in stages:
        value_stage(score_stage(b0))

    tc = 256

    def combine(c, carry):
        rows = pl.ds(pl.multiple_of(c * tc, tc), tc)
        lses = [lse_sc[g, rows, :] for g in range(len(groups))]
        top = functools.reduce(jnp.maximum, lses)
        wts = [jnp.exp2(l - top) for l in lses]
        num = functools.reduce(lambda a, b: a + b,
                               [wt * out_sc[g, rows, :] for g, wt in enumerate(wts)])
        o = num / functools.reduce(lambda a, b: a + b, wts)
        o_ref[0, rows, :] = (o * _silu(z_ref[0, rows, :].astype(jnp.float32))).astype(o_ref.dtype)
        return carry

    lax.fori_loop(0, seq // tc, combine, 0)


def _attention_b(projs, vts, batch, seq):
    hd = B_HEAD_DIM
    nh = projs[1].shape[0] // 2
    assert all(seq % window == 0 for window, _ in B_GROUPS)
    slab = lambda off: pl.BlockSpec((1, seq, hd), lambda b, h: (off + h, b, 0))
    vt_spec = pl.BlockSpec((1, hd, seq), lambda b, h: (b, h, 0))
    return pl.pallas_call(
        functools.partial(_attn_b_kernel, seq=seq, n_heads=nh, groups=B_GROUPS),
        grid=(batch, nh),
        in_specs=[slab(0), slab(nh), vt_spec,
                  slab(0), slab(nh), vt_spec,
                  slab(0), slab(nh), vt_spec,
                  slab(2 * nh)],
        out_specs=pl.BlockSpec((1, seq, hd), lambda b, h: (h, b, 0)),
        out_shape=jax.ShapeDtypeStruct((nh, batch * seq, hd), jnp.bfloat16),
        scratch_shapes=[pltpu.VMEM((len(B_GROUPS), seq, hd), jnp.float32),
                        pltpu.VMEM((len(B_GROUPS), seq, hd), jnp.float32)],
        compiler_params=_params("parallel", "parallel"),
        name="dilated_attention",
    )(projs[0], projs[0], vts[0], projs[1], projs[1], vts[1], projs[2], projs[2], vts[2], projs[0])


def kernel(x, norm_a, w_in_a, lam_q1, lam_k1, lam_q2, lam_k2, subln_a, w_out_a,
           norm_b, w_in_b, w_out_b, norm_f):
    batch, seq, d = x.shape
    m = batch * seq
    bf16 = jnp.bfloat16
    x2 = x.reshape(m, d)
    tn = 1024
    a_tile = 512

    lam_init = 0.8 - 0.6 * math.exp(-0.3 * 0)
    q_scale_a = A_HEAD_DIM ** -0.5 * LOG2E
    hn, vt_a = _norm_project_transposed(x2, norm_a[0], w_in_a[0], d, 2, "norm_in_proj_a_vt",
                                        tm=a_tile)
    v_blk0, v_blk1 = 2 * d // tn, 3 * d // tn
    col_map_a = lambda j: jnp.where(j < v_blk0, j, j + (v_blk1 - v_blk0))
    proj_a = _project(hn, w_in_a[0], 3 * d, col_map_a, 2 * A_HEAD_DIM, "in_proj_a",
                      n_scaled=d // tn, col_scale=q_scale_a, tn=tn)
    gated_a = _attention_a(proj_a, vt_a, lam_q1[0], lam_k1[0], lam_q2[0], lam_k2[0], subln_a[0],
                           batch, seq, lam_init, tile=a_tile)
    h1, hn1, hn4, hn16 = _out_project_a(gated_a, w_out_a[0].astype(bf16), x2, norm_b[0], batch, seq)

    q_scale_b = B_HEAD_DIM ** -0.5 * LOG2E
    w_in = w_in_b[0]
    blk = d // tn
    gate_blk0 = len(B_GROUPS) * 3 * blk
    vt_tn = 512
    projs, vts = [], []
    for g, hn_g in enumerate((hn1, hn4.reshape(m, d), hn16.reshape(m, d))):
        q_blk0 = g * 3 * blk
        n_blk = 2 * blk + (blk if g == 0 else 0)
        col_map = lambda j, q_blk0=q_blk0: jnp.where(j < 2 * blk, q_blk0 + j, gate_blk0 + j - 2 * blk)
        projs.append(_project(hn_g, w_in, n_blk * tn, col_map, B_HEAD_DIM, f"in_proj_b{g}",
                              n_scaled=blk, col_scale=q_scale_b, tn=tn))
        vts.append(_project_transposed(hn_g, w_in, d, (q_blk0 + 2 * blk) * tn // vt_tn,
                                       f"in_proj_b{g}_vt", tm=seq, tn=vt_tn))
    gated_b = _attention_b(projs, vts, batch, seq)
    out = _out_project_b(gated_b, w_out_b[0].astype(bf16), h1, norm_f)
    return out.reshape(batch, seq, d)
```

```python
import functools
import math

import jax
import jax.numpy as jnp
from jax import lax
from jax.experimental import pallas as pl
from jax.experimental.pallas import tpu as pltpu

A_HEAD_DIM = 128
B_GROUPS = ((128, 1), (512, 4), (2048, 16))
B_HEAD_DIM = 128
RMS_EPS = 1e-6
NEG_BIG = -1e30
LOG2E = math.log2(math.e)
LANES = 128

VMEM_LIMIT_BYTES = 48 * 1024 * 1024

PROJ_TILE = 1024
VT_COL_TILE = 512
ATTN_A_TILE = 512
OUT_A_ROWS = 256
OUT_B_ROWS = 512
COMBINE_ROWS = 512


def _params(*semantics):
    return pltpu.CompilerParams(dimension_semantics=semantics, vmem_limit_bytes=VMEM_LIMIT_BYTES)


def _alibi_slope_log2(h, n_heads):
    return LOG2E * jnp.exp2(jnp.zeros((1, 1), jnp.float32)
                            - (8.0 / n_heads) * (h + 1).astype(jnp.float32))


def _silu(z):
    return z * jax.nn.sigmoid(z)


def _aligned(start, multiple):
    return start if isinstance(start, int) else pl.multiple_of(start, multiple)


def _proj_kernel(a_ref, w_ref, o_ref, w_sc, *, cb, n_scaled, col_scale):
    @pl.when(pl.program_id(1) == 0)
    def _():
        w_sc[...] = w_ref[...].astype(w_sc.dtype)

    acc = jnp.dot(a_ref[...], w_sc[...], preferred_element_type=jnp.float32)
    if n_scaled:
        acc = acc * jnp.where(pl.program_id(0) < n_scaled, col_scale, 1.0)
    for c in range(o_ref.shape[0]):
        o_ref[c] = acc[:, c * cb:(c + 1) * cb].astype(o_ref.dtype)


def _project(a, w, n_out, col_block_map, cb, name, n_scaled=0, col_scale=1.0,
             tm=PROJ_TILE, tn=PROJ_TILE):
    m, k = a.shape
    return pl.pallas_call(
        functools.partial(_proj_kernel, cb=cb, n_scaled=n_scaled, col_scale=col_scale),
        grid=(n_out // tn, m // tm),
        in_specs=[pl.BlockSpec((tm, k), lambda j, i: (i, 0)),
                  pl.BlockSpec((k, tn), lambda j, i: (0, col_block_map(j)))],
        out_specs=pl.BlockSpec((tn // cb, tm, cb), lambda j, i: (j, i, 0)),
        out_shape=jax.ShapeDtypeStruct((n_out // cb, m, cb), jnp.bfloat16),
        scratch_shapes=[pltpu.VMEM((k, tn), jnp.bfloat16)],
        compiler_params=_params("arbitrary", "arbitrary"),
        name=name,
    )(a, w)


def _norm_proj_t_kernel(x_ref, g_ref, w_ref, hn_ref, o_ref, wt_sc):
    @pl.when(pl.program_id(0) == 0)
    def _():
        wt_sc[...] = w_ref[...].T.astype(wt_sc.dtype)

    x = x_ref[...]
    ms = jnp.mean(x * x, axis=-1, keepdims=True)
    hn = (x * lax.rsqrt(ms + RMS_EPS) * g_ref[...]).astype(hn_ref.dtype)
    hn_ref[...] = hn
    o_ref[0] = lax.dot_general(wt_sc[...], hn, (((1,), (1,)), ((), ())),
                               preferred_element_type=jnp.float32).astype(o_ref.dtype)


def _norm_project_transposed(x, gain, w, n_out, col_block0, name, tm):
    m, k = x.shape
    const = lambda shape, index: pl.BlockSpec(shape, lambda i: index, pipeline_mode=pl.Buffered(1))
    return pl.pallas_call(
        _norm_proj_t_kernel,
        grid=(m // tm,),
        in_specs=[pl.BlockSpec((tm, k), lambda i: (i, 0)),
                  const((1, k), (0, 0)),
                  const((k, n_out), (0, col_block0))],
        out_specs=[pl.BlockSpec((tm, k), lambda i: (i, 0)),
                   pl.BlockSpec((1, n_out, tm), lambda i: (i, 0, 0))],
        out_shape=[jax.ShapeDtypeStruct((m, k), jnp.bfloat16),
                   jax.ShapeDtypeStruct((m // tm, n_out, tm), jnp.bfloat16)],
        scratch_shapes=[pltpu.VMEM((n_out, k), jnp.bfloat16)],
        compiler_params=_params("arbitrary"),
        name=name,
    )(x, gain.reshape(1, k), w)


def _proj_t_kernel(a_ref, w_ref, o_ref, wt_sc):
    @pl.when(pl.program_id(1) == 0)
    def _():
        wt_sc[...] = w_ref[...].T.astype(wt_sc.dtype)

    o_ref[0] = lax.dot_general(wt_sc[...], a_ref[...], (((1,), (1,)), ((), ())),
                               preferred_element_type=jnp.float32).astype(o_ref.dtype)


def _project_transposed(a, w, n_out, col_block0, name, tm, tn=512):
    m, k = a.shape
    return pl.pallas_call(
        _proj_t_kernel,
        grid=(n_out // tn, m // tm),
        in_specs=[pl.BlockSpec((tm, k), lambda j, i: (i, 0)),
                  pl.BlockSpec((k, tn), lambda j, i: (0, col_block0 + j))],
        out_specs=pl.BlockSpec((1, tn, tm), lambda j, i: (i, j, 0)),
        out_shape=jax.ShapeDtypeStruct((m // tm, n_out, tm), jnp.bfloat16),
        scratch_shapes=[pltpu.VMEM((tn, k), jnp.bfloat16)],
        compiler_params=_params("arbitrary", "arbitrary"),
        name=name,
    )(a, w)


def _gather_slabs(a_ref):
    return jnp.concatenate([a_ref[c] for c in range(a_ref.shape[0])], axis=-1)


def _out_proj_a_kernel(a_ref, w_ref, r_ref, g_ref, h_ref, hn1_ref, hn4_ref, hn16_ref,
                       slab_sc, slab4_sc):
    y = jnp.dot(_gather_slabs(a_ref), w_ref[...], preferred_element_type=jnp.float32) + r_ref[...]
    h_ref[...] = y
    ms = jnp.mean(y * y, axis=-1, keepdims=True)
    hn = y * lax.rsqrt(ms + RMS_EPS) * g_ref[...]
    hn1_ref[...] = hn.astype(hn1_ref.dtype)
    tm, d = hn.shape
    n_slab = d // LANES
    for c in range(n_slab):
        slab_sc[c] = hn[:, c * LANES:(c + 1) * LANES]
    dil, dil2 = hn4_ref.shape[1], hn16_ref.shape[1]
    step2 = dil2 // dil
    assert dil * step2 == dil2
    n4 = tm // dil
    for r in range(dil):
        pieces = [slab_sc[c, pl.ds(r, n4, stride=dil), :] for c in range(n_slab)]
        hn4_ref[0, r] = jnp.concatenate(pieces, axis=-1).astype(hn4_ref.dtype)
        for c in range(n_slab):
            slab4_sc[c, r * n4:(r + 1) * n4, :] = pieces[c]
    for r in range(dil2):
        rows = jnp.concatenate(
            [slab4_sc[c, pl.ds((r % dil) * n4 + r // dil, tm // dil2, stride=step2), :]
             for c in range(n_slab)], axis=-1)
        hn16_ref[0, r] = rows.astype(hn16_ref.dtype)


def _out_project_a(a, w, resid, gain, batch, seq, tm=OUT_A_ROWS):
    nc, m, cb = a.shape
    k, n = w.shape
    nb = seq // tm
    const = lambda shape: pl.BlockSpec(shape, lambda i: (0,) * len(shape),
                                       pipeline_mode=pl.Buffered(1))
    perm_spec = lambda dil: pl.BlockSpec((1, dil, tm // dil, n), lambda i: (i // nb, 0, i % nb, 0))
    perm_shape = lambda dil: jax.ShapeDtypeStruct((batch, dil, seq // dil, n), jnp.bfloat16)
    return pl.pallas_call(
        _out_proj_a_kernel,
        grid=(m // tm,),
        in_specs=[pl.BlockSpec((nc, tm, cb), lambda i: (0, i, 0)),
                  const((k, n)),
                  pl.BlockSpec((tm, n), lambda i: (i, 0)),
                  const((1, n))],
        out_specs=[pl.BlockSpec((tm, n), lambda i: (i, 0)),
                   pl.BlockSpec((tm, n), lambda i: (i, 0)),
                   perm_spec(4), perm_spec(16)],
        out_shape=[jax.ShapeDtypeStruct((m, n), jnp.float32),
                   jax.ShapeDtypeStruct((m, n), jnp.bfloat16),
                   perm_shape(4), perm_shape(16)],
        scratch_shapes=[pltpu.VMEM((n // LANES, tm, LANES), jnp.float32)] * 2,
        compiler_params=_params("parallel"),
        name="out_proj_a",
    )(a, w, resid, gain.reshape(1, n))


def _out_proj_b_kernel(a_ref, w_ref, r_ref, g_ref, o_ref):
    y = jnp.dot(_gather_slabs(a_ref), w_ref[...], preferred_element_type=jnp.float32) + r_ref[...]
    ms = jnp.mean(y * y, axis=-1, keepdims=True)
    o_ref[...] = y * lax.rsqrt(ms + RMS_EPS) * g_ref[...]


def _out_project_b(a, w, resid, gain, tm=OUT_B_ROWS):
    nc, m, cb = a.shape
    k, n = w.shape
    const = lambda shape: pl.BlockSpec(shape, lambda i: (0,) * len(shape),
                                       pipeline_mode=pl.Buffered(1))
    return pl.pallas_call(
        _out_proj_b_kernel,
        grid=(m // tm,),
        in_specs=[pl.BlockSpec((nc, tm, cb), lambda i: (0, i, 0)),
                  const((k, n)),
                  pl.BlockSpec((tm, n), lambda i: (i, 0)),
                  const((1, n))],
        out_specs=pl.BlockSpec((tm, n), lambda i: (i, 0)),
        out_shape=jax.ShapeDtypeStruct((m, n), jnp.float32),
        compiler_params=_params("parallel"),
        name="out_proj_b",
    )(a, w, resid, gain.reshape(1, n))


_SLOPE_PIECES = 3
_POS_RADIX = 64

def _attn_a_kernel(q_ref, k_ref, vt_ref, z_ref, lq1_ref, lk1_ref, lq2_ref, lk2_ref, sub_ref,
                   o_ref, s_sc, kfeat_sc, mask_sc, *, tile, n_heads, lam_init):
    hd = A_HEAD_DIM
    seq = q_ref.shape[1]
    nt = seq // tile
    h = pl.program_id(1)
    slope = _alibi_slope_log2(h, n_heads)
    lam = (jnp.exp(jnp.sum(lq1_ref[...] * lk1_ref[...], axis=-1, keepdims=True))
           - jnp.exp(jnp.sum(lq2_ref[...] * lk2_ref[...], axis=-1, keepdims=True)) + lam_init)

    bf16 = k_ref.dtype
    col = lax.broadcasted_iota(jnp.int32, (tile, hd), 1)
    row = lax.broadcasted_iota(jnp.int32, (tile, hd), 0)
    pieces, rest = [], slope
    for _ in range(_SLOPE_PIECES):
        piece = rest.astype(bf16).astype(jnp.float32)
        pieces.append(piece)
        rest = rest - piece
    q_feat = jnp.zeros((tile, hd), jnp.float32)
    for n, piece in enumerate(pieces):
        q_feat = jnp.where(col == n, _POS_RADIX * piece, q_feat)
        q_feat = jnp.where(col == _SLOPE_PIECES + n, piece, q_feat)
    q_feat = q_feat.astype(bf16)
    for kt in range(nt):
        pos = row + kt * tile
        hi = (pos // _POS_RADIX).astype(jnp.float32)
        lo = (pos % _POS_RADIX).astype(jnp.float32)
        kfeat_sc[kt] = jnp.where(col < _SLOPE_PIECES, hi,
                                 jnp.where(col < 2 * _SLOPE_PIECES, lo, 0.0)).astype(bf16)
    half = tile // 2
    mask_sc[...] = jnp.where(lax.broadcasted_iota(jnp.int32, (half, half), 1)
                             >= lax.broadcasted_iota(jnp.int32, (half, half), 0), 0.0, -jnp.inf)

    def panels(qt):
        k0 = qt * tile
        return ([((kt * tile, (kt + 1) * tile), (0, tile), None) for kt in range(qt)]
                + [((k0, k0 + half), (0, half), 0), ((k0, k0 + tile), (half, tile), half)])

    def scores(qt):
        q = q_ref[0, pl.ds(qt * tile, tile), :]
        for i in range(2):
            q_aug = jnp.concatenate([q[:, i * hd:(i + 1) * hd], q_feat], axis=1)
            maxima = [None, None]
            for (k0, k1), (c0, c1), mask_row in panels(qt):
                k_aug = jnp.concatenate([k_ref[0, k0:k1, i * hd:(i + 1) * hd],
                                         kfeat_sc[k0 // tile, k0 % tile:k0 % tile + k1 - k0, :]], axis=1)
                s = lax.dot_general(k_aug, q_aug[c0:c1], (((1,), (1,)), ((), ())),
                                    preferred_element_type=jnp.float32)
                if mask_row == 0:
                    s = s + mask_sc[...]
                elif mask_row is not None:
                    s = jnp.concatenate([s[:mask_row], s[mask_row:] + mask_sc[...]], axis=0)
                s_sc[qt % 2, i, k0:k1, c0:c1] = s
                top = jnp.max(s, axis=0, keepdims=True)
                for hf in range(2):
                    lo, hi = max(c0, hf * half), min(c1, (hf + 1) * half)
                    if lo < hi:
                        piece = top[:, lo - c0:hi - c0]
                        maxima[hf] = piece if maxima[hf] is None else jnp.maximum(maxima[hf], piece)
            col_max[qt, i] = jnp.concatenate(maxima, axis=1)

    col_max = {}
    sub_gain = sub_ref[...] * (1.0 - lam_init)

    def finish(qt):
        heads = []
        for i in range(2):
            top = col_max.pop((qt, i))
            sums = [None, None]
            out = [None, None]
            for (k0, k1), (c0, c1), _ in panels(qt):
                p = jnp.exp2(s_sc[qt % 2, i, k0:k1, c0:c1] - top[:, c0:c1])
                part = jnp.sum(p, axis=0, keepdims=True)
                vtc = vt_ref[k0 // tile][:, k0 % tile:k0 % tile + k1 - k0]
                pv = jnp.dot(vtc, p.astype(vtc.dtype), preferred_element_type=jnp.float32)
                for hf in range(2):
                    lo, hi = max(c0, hf * half), min(c1, (hf + 1) * half)
                    if lo < hi:
                        add = lambda old, new: new if old is None else old + new
                        sums[hf] = add(sums[hf], part[:, lo - c0:hi - c0])
                        out[hf] = add(out[hf], pv[:, lo - c0:hi - c0])
            heads.append(jnp.concatenate(out, axis=1) / jnp.concatenate(sums, axis=1))
        rows = pl.ds(qt * tile, tile)
        o = (heads[0] - lam * heads[1]).T
        ms = jnp.mean(o * o, axis=-1, keepdims=True)
        o = o * lax.rsqrt(ms + RMS_EPS) * sub_gain
        o_ref[0, rows, :] = (o * _silu(z_ref[0, rows, :].astype(jnp.float32))).astype(o_ref.dtype)

    scores(0)
    for qt in range(nt):
        if qt + 1 < nt:
            scores(qt + 1)
        finish(qt)


def _attention_a(proj, vt, lam_q1, lam_k1, lam_q2, lam_k2, subln, batch, seq, lam_init, tile):
    w2 = 2 * A_HEAD_DIM
    nh = proj.shape[0] // 3
    nt = seq // tile
    assert vt.shape[2] == tile
    vec = lambda a: a.reshape(1, -1)
    vec_spec = pl.BlockSpec((1, A_HEAD_DIM), lambda b, h: (0, 0))
    slab = lambda off: pl.BlockSpec((1, seq, w2), lambda b, h: (off + h, b, 0))
    return pl.pallas_call(
        functools.partial(_attn_a_kernel, tile=tile, n_heads=nh, lam_init=lam_init),
        grid=(batch, nh),
        in_specs=[slab(0), slab(nh),
                  pl.BlockSpec((nt, w2, tile), lambda b, h: (b, h, 0)),
                  slab(2 * nh),
                  vec_spec, vec_spec, vec_spec, vec_spec,
                  pl.BlockSpec((1, w2), lambda b, h: (0, 0))],
        out_specs=slab(0),
        out_shape=jax.ShapeDtypeStruct((nh, batch * seq, w2), jnp.bfloat16),
        scratch_shapes=[pltpu.VMEM((2, 2, seq, tile), jnp.float32),
                        pltpu.VMEM((nt, tile, A_HEAD_DIM), jnp.bfloat16),
                        pltpu.VMEM((tile // 2, tile // 2), jnp.float32)],
        compiler_params=_params("parallel", "arbitrary"),
        name="diff_attention",
    )(proj, proj, vt, proj, vec(lam_q1), vec(lam_k1), vec(lam_q2), vec(lam_k2), vec(subln))


_B_BATCH = 8


def _attn_b_kernel(q0_ref, k0_ref, v0_ref, q1_ref, k1_ref, v1_ref, q2_ref, k2_ref, v2_ref, z_ref,
                   o_ref, out_sc, lse_sc, *, seq, n_heads, groups):
    hd = B_HEAD_DIM
    h = pl.program_id(1)
    slope = _alibi_slope_log2(h, n_heads)
    qkv = ((q0_ref, k0_ref, v0_ref), (q1_ref, k1_ref, v1_ref), (q2_ref, k2_ref, v2_ref))

    stages = []
    for g, (window, dil) in enumerate(groups):
        w = window // dil
        n_per = (seq // dil) // w
        q_ref, k_ref, v_ref = qkv[g]

        def band_bias(n_keys, w=w, dil=dil):
            jj = lax.broadcasted_iota(jnp.int32, (n_keys, w), 0)
            ii = lax.broadcasted_iota(jnp.int32, (n_keys, w), 1)
            rel = ii + (n_keys - w) - jj
            return jnp.where((rel >= 0) & (rel <= w),
                             -slope * (rel * dil).astype(jnp.float32), NEG_BIG)

        biases = {w: band_bias(w)}
        if n_per > 1:
            biases[2 * w] = band_bias(2 * w)
        n_blocks = seq // w

        def score_stage(b0, w=w, n_per=n_per, n_blocks=n_blocks, biases=biases,
                        q_ref=q_ref, k_ref=k_ref):
            blocks = []
            for n in range(b0, min(b0 + _B_BATCH, n_blocks)):
                res, nb = divmod(n, n_per)
                n_keys = w if nb == 0 else 2 * w
                key_rows = pl.ds((n + 1) * w - n_keys, n_keys)
                q = q_ref[0, pl.ds(n * w, w), :]
                s = lax.dot_general(k_ref[0, key_rows, :], q, (((1,), (1,)), ((), ())),
                                    preferred_element_type=jnp.float32) + biases[n_keys]
                blocks.append((res, nb, key_rows, s))
            return blocks

        def value_stage(blocks, w=w, dil=dil, g=g, v_ref=v_ref):
            soft = []
            for res, nb, key_rows, s in blocks:
                m = jnp.max(s, axis=0, keepdims=True)
                p = jnp.exp2(s - m)
                den = jnp.sum(p, axis=0, keepdims=True)
                soft.append((p.astype(v_ref.dtype), den, m + jnp.log2(den)))
            for (res, nb, key_rows, _), (p, den, lse) in zip(blocks, soft):
                o_t = jnp.dot(v_ref[0, :, key_rows], p, preferred_element_type=jnp.float32) / den
                t0 = nb * (w * dil) + res
                dst = pl.ds(t0, w, stride=dil) if dil > 1 else pl.ds(t0, w)
                out_sc[g, dst, :] = o_t.T
                lse_sc[g, dst, :] = jnp.broadcast_to(lse, (hd, w)).T

        stages += [(score_stage, b0, value_stage) for b0 in range(0, n_blocks, _B_BATCH)]

    for score_stage, b0, value_stage in stages:
        value_stage(score_stage(b0))

    tc = COMBINE_ROWS

    def combine(c, carry):
        rows = pl.ds(pl.multiple_of(c * tc, tc), tc)
        lses = [lse_sc[g, rows, :] for g in range(len(groups))]
        top = functools.reduce(jnp.maximum, lses)
        wts = [jnp.exp2(l - top) for l in lses]
        num = functools.reduce(lambda a, b: a + b,
                               [wt * out_sc[g, rows, :] for g, wt in enumerate(wts)])
        den = functools.reduce(lambda a, b: a + b, wts)
        z = z_ref[0, rows, :].astype(jnp.float32)
        o_ref[0, rows, :] = (num * z / (den * (1.0 + jnp.exp(-z)))).astype(o_ref.dtype)
        return carry

    lax.fori_loop(0, seq // tc, combine, 0)


def _attention_b(projs, vts, batch, seq):
    hd = B_HEAD_DIM
    nh = projs[1].shape[0] // 2
    assert all(seq % window == 0 for window, _ in B_GROUPS)
    slab = lambda off: pl.BlockSpec((1, seq, hd), lambda b, h: (off + h, b, 0))
    vt_spec = pl.BlockSpec((1, hd, seq), lambda b, h: (b, h, 0))
    return pl.pallas_call(
        functools.partial(_attn_b_kernel, seq=seq, n_heads=nh, groups=B_GROUPS),
        grid=(batch, nh),
        in_specs=[slab(0), slab(nh), vt_spec,
                  slab(0), slab(nh), vt_spec,
                  slab(0), slab(nh), vt_spec,
                  slab(2 * nh)],
        out_specs=pl.BlockSpec((1, seq, hd), lambda b, h: (h, b, 0)),
        out_shape=jax.ShapeDtypeStruct((nh, batch * seq, hd), jnp.bfloat16),
        scratch_shapes=[pltpu.VMEM((len(B_GROUPS), seq, hd), jnp.float32),
                        pltpu.VMEM((len(B_GROUPS), seq, hd), jnp.float32)],
        compiler_params=_params("parallel", "parallel"),
        name="dilated_attention",
    )(projs[0], projs[0], vts[0], projs[1], projs[1], vts[1], projs[2], projs[2], vts[2], projs[0])


def kernel(x, norm_a, w_in_a, lam_q1, lam_k1, lam_q2, lam_k2, subln_a, w_out_a,
           norm_b, w_in_b, w_out_b, norm_f):
    batch, seq, d = x.shape
    m = batch * seq
    bf16 = jnp.bfloat16
    x2 = x.reshape(m, d)
    tn = PROJ_TILE
    a_tile = ATTN_A_TILE

    lam_init = 0.8 - 0.6 * math.exp(-0.3 * 0)
    q_scale_a = A_HEAD_DIM ** -0.5 * LOG2E
    hn, vt_a = _norm_project_transposed(x2, norm_a[0], w_in_a[0], d, 2, "norm_in_proj_a_vt",
                                        tm=a_tile)
    v_blk0, v_blk1 = 2 * d // tn, 3 * d // tn
    col_map_a = lambda j: jnp.where(j < v_blk0, j, j + (v_blk1 - v_blk0))
    proj_a = _project(hn, w_in_a[0], 3 * d, col_map_a, 2 * A_HEAD_DIM, "in_proj_a",
                      n_scaled=d // tn, col_scale=q_scale_a, tn=tn)
    gated_a = _attention_a(proj_a, vt_a, lam_q1[0], lam_k1[0], lam_q2[0], lam_k2[0], subln_a[0],
                           batch, seq, lam_init, tile=a_tile)
    h1, hn1, hn4, hn16 = _out_project_a(gated_a, w_out_a[0].astype(bf16), x2, norm_b[0], batch, seq)

    q_scale_b = B_HEAD_DIM ** -0.5 * LOG2E
    w_in = w_in_b[0]
    blk = d // tn
    gate_blk0 = len(B_GROUPS) * 3 * blk
    vt_tn = VT_COL_TILE
    projs, vts = [], []
    for g, hn_g in enumerate((hn1, hn4.reshape(m, d), hn16.reshape(m, d))):
        q_blk0 = g * 3 * blk
        n_blk = 2 * blk + (blk if g == 0 else 0)
        col_map = lambda j, q_blk0=q_blk0: jnp.where(j < 2 * blk, q_blk0 + j, gate_blk0 + j - 2 * blk)
        projs.append(_project(hn_g, w_in, n_blk * tn, col_map, B_HEAD_DIM, f"in_proj_b{g}",
                              n_scaled=blk, col_scale=q_scale_b, tn=tn))
        vts.append(_project_transposed(hn_g, w_in, d, (q_blk0 + 2 * blk) * tn // vt_tn,
                                       f"in_proj_b{g}_vt", tm=seq, tn=vt_tn))
    gated_b = _attention_b(projs, vts, batch, seq)
    out = _out_project_b(gated_b, w_out_b[0].astype(bf16), h1, norm_f)
    return out.reshape(batch, seq, d)
```

```python
import functools
import math

import jax
import jax.numpy as jnp
from jax import lax
from jax.experimental import pallas as pl
from jax.experimental.pallas import tpu as pltpu

A_HEAD_DIM = 128
B_GROUPS = ((128, 1), (512, 4), (2048, 16))
B_HEAD_DIM = 128
RMS_EPS = 1e-6
NEG_BIG = -1e30
LOG2E = math.log2(math.e)
LANES = 128

VMEM_LIMIT_BYTES = 48 * 1024 * 1024
PROJ_VMEM_LIMIT_BYTES = 58 * 1024 * 1024

PROJ_TILE = 1024
PROJ_ROWS = 2048
VT_COL_TILE = 512
ATTN_A_TILE = 512
OUT_A_ROWS = 256
OUT_B_ROWS = 512
COMBINE_ROWS = 512


def _params(*semantics, vmem_limit_bytes=VMEM_LIMIT_BYTES):
    return pltpu.CompilerParams(dimension_semantics=semantics, vmem_limit_bytes=vmem_limit_bytes)


def _alibi_slope_log2(h, n_heads):
    return LOG2E * jnp.exp2(jnp.zeros((1, 1), jnp.float32)
                            - (8.0 / n_heads) * (h + 1).astype(jnp.float32))


def _silu(z):
    return z * jax.nn.sigmoid(z)


def _aligned(start, multiple):
    return start if isinstance(start, int) else pl.multiple_of(start, multiple)


def _proj_kernel(a_ref, w_ref, o_ref, w_sc, *, cb, n_scaled, col_scale):
    @pl.when(pl.program_id(1) == 0)
    def _():
        w_sc[...] = w_ref[...].astype(w_sc.dtype)

    acc = jnp.dot(a_ref[...], w_sc[...], preferred_element_type=jnp.float32)
    if n_scaled:
        acc = acc * jnp.where(pl.program_id(0) < n_scaled, col_scale, 1.0)
    for c in range(o_ref.shape[0]):
        o_ref[c] = acc[:, c * cb:(c + 1) * cb].astype(o_ref.dtype)


def _project(a, w, n_out, col_block_map, cb, name, n_scaled=0, col_scale=1.0,
             tm=PROJ_ROWS, tn=PROJ_TILE):
    m, k = a.shape
    return pl.pallas_call(
        functools.partial(_proj_kernel, cb=cb, n_scaled=n_scaled, col_scale=col_scale),
        grid=(n_out // tn, m // tm),
        in_specs=[pl.BlockSpec((tm, k), lambda j, i: (i, 0)),
                  pl.BlockSpec((k, tn), lambda j, i: (0, col_block_map(j)))],
        out_specs=pl.BlockSpec((tn // cb, tm, cb), lambda j, i: (j, i, 0)),
        out_shape=jax.ShapeDtypeStruct((n_out // cb, m, cb), jnp.bfloat16),
        scratch_shapes=[pltpu.VMEM((k, tn), jnp.bfloat16)],
        compiler_params=_params("arbitrary", "arbitrary", vmem_limit_bytes=PROJ_VMEM_LIMIT_BYTES),
        name=name,
    )(a, w)


def _norm_proj_t_kernel(x_ref, g_ref, w_ref, hn_ref, o_ref, wt_sc):
    @pl.when(pl.program_id(0) == 0)
    def _():
        wt_sc[...] = w_ref[...].T.astype(wt_sc.dtype)

    x = x_ref[...]
    ms = jnp.mean(x * x, axis=-1, keepdims=True)
    hn = (x * lax.rsqrt(ms + RMS_EPS) * g_ref[...]).astype(hn_ref.dtype)
    hn_ref[...] = hn
    o_ref[0] = lax.dot_general(wt_sc[...], hn, (((1,), (1,)), ((), ())),
                               preferred_element_type=jnp.float32).astype(o_ref.dtype)


def _norm_project_transposed(x, gain, w, n_out, col_block0, name, tm):
    m, k = x.shape
    const = lambda shape, index: pl.BlockSpec(shape, lambda i: index, pipeline_mode=pl.Buffered(1))
    return pl.pallas_call(
        _norm_proj_t_kernel,
        grid=(m // tm,),
        in_specs=[pl.BlockSpec((tm, k), lambda i: (i, 0)),
                  const((1, k), (0, 0)),
                  const((k, n_out), (0, col_block0))],
        out_specs=[pl.BlockSpec((tm, k), lambda i: (i, 0)),
                   pl.BlockSpec((1, n_out, tm), lambda i: (i, 0, 0))],
        out_shape=[jax.ShapeDtypeStruct((m, k), jnp.bfloat16),
                   jax.ShapeDtypeStruct((m // tm, n_out, tm), jnp.bfloat16)],
        scratch_shapes=[pltpu.VMEM((n_out, k), jnp.bfloat16)],
        compiler_params=_params("arbitrary"),
        name=name,
    )(x, gain.reshape(1, k), w)


def _proj_t_kernel(a_ref, w_ref, o_ref, wt_sc):
    @pl.when(pl.program_id(1) == 0)
    def _():
        wt_sc[...] = w_ref[...].T.astype(wt_sc.dtype)

    o_ref[0] = lax.dot_general(wt_sc[...], a_ref[...], (((1,), (1,)), ((), ())),
                               preferred_element_type=jnp.float32).astype(o_ref.dtype)


def _project_transposed(a, w, n_out, col_block0, name, tm, tn=512):
    m, k = a.shape
    return pl.pallas_call(
        _proj_t_kernel,
        grid=(n_out // tn, m // tm),
        in_specs=[pl.BlockSpec((tm, k), lambda j, i: (i, 0)),
                  pl.BlockSpec((k, tn), lambda j, i: (0, col_block0 + j))],
        out_specs=pl.BlockSpec((1, tn, tm), lambda j, i: (i, j, 0)),
        out_shape=jax.ShapeDtypeStruct((m // tm, n_out, tm), jnp.bfloat16),
        scratch_shapes=[pltpu.VMEM((tn, k), jnp.bfloat16)],
        compiler_params=_params("arbitrary", "arbitrary"),
        name=name,
    )(a, w)


def _gather_slabs(a_ref):
    return jnp.concatenate([a_ref[c] for c in range(a_ref.shape[0])], axis=-1)


def _out_proj_a_kernel(a_ref, w_ref, r_ref, g_ref, h_ref, hn1_ref, hn4_ref, hn16_ref,
                       slab_sc, slab4_sc):
    y = jnp.dot(_gather_slabs(a_ref), w_ref[...], preferred_element_type=jnp.float32) + r_ref[...]
    h_ref[...] = y
    ms = jnp.mean(y * y, axis=-1, keepdims=True)
    hn = y * lax.rsqrt(ms + RMS_EPS) * g_ref[...]
    hn1_ref[...] = hn.astype(hn1_ref.dtype)
    tm, d = hn.shape
    n_slab = d // LANES
    for c in range(n_slab):
        slab_sc[c] = hn[:, c * LANES:(c + 1) * LANES]
    dil, dil2 = hn4_ref.shape[1], hn16_ref.shape[1]
    step2 = dil2 // dil
    assert dil * step2 == dil2
    n4 = tm // dil
    for r in range(dil):
        pieces = [slab_sc[c, pl.ds(r, n4, stride=dil), :] for c in range(n_slab)]
        hn4_ref[0, r] = jnp.concatenate(pieces, axis=-1).astype(hn4_ref.dtype)
        for c in range(n_slab):
            slab4_sc[c, r * n4:(r + 1) * n4, :] = pieces[c]
    for r in range(dil2):
        rows = jnp.concatenate(
            [slab4_sc[c, pl.ds((r % dil) * n4 + r // dil, tm // dil2, stride=step2), :]
             for c in range(n_slab)], axis=-1)
        hn16_ref[0, r] = rows.astype(hn16_ref.dtype)


def _out_project_a(a, w, resid, gain, batch, seq, tm=OUT_A_ROWS):
    nc, m, cb = a.shape
    k, n = w.shape
    nb = seq // tm
    const = lambda shape: pl.BlockSpec(shape, lambda i: (0,) * len(shape),
                                       pipeline_mode=pl.Buffered(1))
    perm_spec = lambda dil: pl.BlockSpec((1, dil, tm // dil, n), lambda i: (i // nb, 0, i % nb, 0))
    perm_shape = lambda dil: jax.ShapeDtypeStruct((batch, dil, seq // dil, n), jnp.bfloat16)
    return pl.pallas_call(
        _out_proj_a_kernel,
        grid=(m // tm,),
        in_specs=[pl.BlockSpec((nc, tm, cb), lambda i: (0, i, 0)),
                  const((k, n)),
                  pl.BlockSpec((tm, n), lambda i: (i, 0)),
                  const((1, n))],
        out_specs=[pl.BlockSpec((tm, n), lambda i: (i, 0)),
                   pl.BlockSpec((tm, n), lambda i: (i, 0)),
                   perm_spec(4), perm_spec(16)],
        out_shape=[jax.ShapeDtypeStruct((m, n), jnp.float32),
                   jax.ShapeDtypeStruct((m, n), jnp.bfloat16),
                   perm_shape(4), perm_shape(16)],
        scratch_shapes=[pltpu.VMEM((n // LANES, tm, LANES), jnp.float32)] * 2,
        compiler_params=_params("parallel"),
        name="out_proj_a",
    )(a, w, resid, gain.reshape(1, n))


def _out_proj_b_kernel(a_ref, w_ref, r_ref, g_ref, o_ref):
    y = jnp.dot(_gather_slabs(a_ref), w_ref[...], preferred_element_type=jnp.float32) + r_ref[...]
    ms = jnp.mean(y * y, axis=-1, keepdims=True)
    o_ref[...] = y * lax.rsqrt(ms + RMS_EPS) * g_ref[...]


def _out_project_b(a, w, resid, gain, tm=OUT_B_ROWS):
    nc, m, cb = a.shape
    k, n = w.shape
    const = lambda shape: pl.BlockSpec(shape, lambda i: (0,) * len(shape),
                                       pipeline_mode=pl.Buffered(1))
    return pl.pallas_call(
        _out_proj_b_kernel,
        grid=(m // tm,),
        in_specs=[pl.BlockSpec((nc, tm, cb), lambda i: (0, i, 0)),
                  const((k, n)),
                  pl.BlockSpec((tm, n), lambda i: (i, 0)),
                  const((1, n))],
        out_specs=pl.BlockSpec((tm, n), lambda i: (i, 0)),
        out_shape=jax.ShapeDtypeStruct((m, n), jnp.float32),
        compiler_params=_params("parallel"),
        name="out_proj_b",
    )(a, w, resid, gain.reshape(1, n))


_SLOPE_PIECES = 3
_POS_RADIX = 64

def _attn_a_kernel(q_ref, k_ref, vt_ref, z_ref, lq1_ref, lk1_ref, lq2_ref, lk2_ref, sub_ref,
                   o_ref, s_sc, kfeat_sc, mask_sc, *, tile, n_heads, lam_init):
    hd = A_HEAD_DIM
    seq = q_ref.shape[1]
    nt = seq // tile
    h = pl.program_id(1)
    slope = _alibi_slope_log2(h, n_heads)
    lam = (jnp.exp(jnp.sum(lq1_ref[...] * lk1_ref[...], axis=-1, keepdims=True))
           - jnp.exp(jnp.sum(lq2_ref[...] * lk2_ref[...], axis=-1, keepdims=True)) + lam_init)

    bf16 = k_ref.dtype
    col = lax.broadcasted_iota(jnp.int32, (tile, hd), 1)
    row = lax.broadcasted_iota(jnp.int32, (tile, hd), 0)
    pieces, rest = [], slope
    for _ in range(_SLOPE_PIECES):
        piece = rest.astype(bf16).astype(jnp.float32)
        pieces.append(piece)
        rest = rest - piece
    q_feat = jnp.zeros((tile, hd), jnp.float32)
    for n, piece in enumerate(pieces):
        q_feat = jnp.where(col == n, _POS_RADIX * piece, q_feat)
        q_feat = jnp.where(col == _SLOPE_PIECES + n, piece, q_feat)
    q_feat = q_feat.astype(bf16)
    for kt in range(nt):
        pos = row + kt * tile
        hi = (pos // _POS_RADIX).astype(jnp.float32)
        lo = (pos % _POS_RADIX).astype(jnp.float32)
        kfeat_sc[kt] = jnp.where(col < _SLOPE_PIECES, hi,
                                 jnp.where(col < 2 * _SLOPE_PIECES, lo, 0.0)).astype(bf16)
    half = tile // 2
    mask_sc[...] = jnp.where(lax.broadcasted_iota(jnp.int32, (half, half), 1)
                             >= lax.broadcasted_iota(jnp.int32, (half, half), 0), 0.0, -jnp.inf)

    def panels(qt):
        k0 = qt * tile
        return ([((kt * tile, (kt + 1) * tile), (0, tile), None) for kt in range(qt)]
                + [((k0, k0 + half), (0, half), 0), ((k0, k0 + tile), (half, tile), half)])

    def scores(qt):
        q = q_ref[0, pl.ds(qt * tile, tile), :]
        for i in range(2):
            q_aug = jnp.concatenate([q[:, i * hd:(i + 1) * hd], q_feat], axis=1)
            maxima = [None, None]
            for (k0, k1), (c0, c1), mask_row in panels(qt):
                k_aug = jnp.concatenate([k_ref[0, k0:k1, i * hd:(i + 1) * hd],
                                         kfeat_sc[k0 // tile, k0 % tile:k0 % tile + k1 - k0, :]], axis=1)
                s = lax.dot_general(k_aug, q_aug[c0:c1], (((1,), (1,)), ((), ())),
                                    preferred_element_type=jnp.float32)
                if mask_row == 0:
                    s = s + mask_sc[...]
                elif mask_row is not None:
                    s = jnp.concatenate([s[:mask_row], s[mask_row:] + mask_sc[...]], axis=0)
                s_sc[qt % 2, i, k0:k1, c0:c1] = s
                top = jnp.max(s, axis=0, keepdims=True)
                for hf in range(2):
                    lo, hi = max(c0, hf * half), min(c1, (hf + 1) * half)
                    if lo < hi:
                        piece = top[:, lo - c0:hi - c0]
                        maxima[hf] = piece if maxima[hf] is None else jnp.maximum(maxima[hf], piece)
            col_max[qt, i] = jnp.concatenate(maxima, axis=1)

    col_max = {}
    sub_gain = sub_ref[...] * (1.0 - lam_init)

    def finish(qt):
        heads = []
        for i in range(2):
            top = col_max.pop((qt, i))
            sums = [None, None]
            out = [None, None]
            for (k0, k1), (c0, c1), _ in panels(qt):
                p = jnp.exp2(s_sc[qt % 2, i, k0:k1, c0:c1] - top[:, c0:c1])
                part = jnp.sum(p, axis=0, keepdims=True)
                vtc = vt_ref[k0 // tile][:, k0 % tile:k0 % tile + k1 - k0]
                pv = jnp.dot(vtc, p.astype(vtc.dtype), preferred_element_type=jnp.float32)
                for hf in range(2):
                    lo, hi = max(c0, hf * half), min(c1, (hf + 1) * half)
                    if lo < hi:
                        add = lambda old, new: new if old is None else old + new
                        sums[hf] = add(sums[hf], part[:, lo - c0:hi - c0])
                        out[hf] = add(out[hf], pv[:, lo - c0:hi - c0])
            heads.append(jnp.concatenate(out, axis=1) / jnp.concatenate(sums, axis=1))
        rows = pl.ds(qt * tile, tile)
        o = (heads[0] - lam * heads[1]).T
        ms = jnp.mean(o * o, axis=-1, keepdims=True)
        o = o * lax.rsqrt(ms + RMS_EPS) * sub_gain
        o_ref[0, rows, :] = (o * _silu(z_ref[0, rows, :].astype(jnp.float32))).astype(o_ref.dtype)

    scores(0)
    for qt in range(nt):
        if qt + 1 < nt:
            scores(qt + 1)
        finish(qt)


def _attention_a(proj, vt, lam_q1, lam_k1, lam_q2, lam_k2, subln, batch, seq, lam_init, tile):
    w2 = 2 * A_HEAD_DIM
    nh = proj.shape[0] // 3
    nt = seq // tile
    assert vt.shape[2] == tile
    vec = lambda a: a.reshape(1, -1)
    vec_spec = pl.BlockSpec((1, A_HEAD_DIM), lambda b, h: (0, 0))
    slab = lambda off: pl.BlockSpec((1, seq, w2), lambda b, h: (off + h, b, 0))
    return pl.pallas_call(
        functools.partial(_attn_a_kernel, tile=tile, n_heads=nh, lam_init=lam_init),
        grid=(batch, nh),
        in_specs=[slab(0), slab(nh),
                  pl.BlockSpec((nt, w2, tile), lambda b, h: (b, h, 0)),
                  slab(2 * nh),
                  vec_spec, vec_spec, vec_spec, vec_spec,
                  pl.BlockSpec((1, w2), lambda b, h: (0, 0))],
        out_specs=slab(0),
        out_shape=jax.ShapeDtypeStruct((nh, batch * seq, w2), jnp.bfloat16),
        scratch_shapes=[pltpu.VMEM((2, 2, seq, tile), jnp.float32),
                        pltpu.VMEM((nt, tile, A_HEAD_DIM), jnp.bfloat16),
                        pltpu.VMEM((tile // 2, tile // 2), jnp.float32)],
        compiler_params=_params("parallel", "arbitrary"),
        name="diff_attention",
    )(proj, proj, vt, proj, vec(lam_q1), vec(lam_k1), vec(lam_q2), vec(lam_k2), vec(subln))


_B_BATCH = 8


def _attn_b_kernel(q0_ref, k0_ref, v0_ref, q1_ref, k1_ref, v1_ref, q2_ref, k2_ref, v2_ref, z_ref,
                   o_ref, out_sc, lse_sc, *, seq, n_heads, groups):
    hd = B_HEAD_DIM
    h = pl.program_id(1)
    slope = _alibi_slope_log2(h, n_heads)
    qkv = ((q0_ref, k0_ref, v0_ref), (q1_ref, k1_ref, v1_ref), (q2_ref, k2_ref, v2_ref))

    stages = []
    for g, (window, dil) in enumerate(groups):
        w = window // dil
        n_per = (seq // dil) // w
        q_ref, k_ref, v_ref = qkv[g]

        def band_bias(n_keys, w=w, dil=dil):
            jj = lax.broadcasted_iota(jnp.int32, (n_keys, w), 0)
            ii = lax.broadcasted_iota(jnp.int32, (n_keys, w), 1)
            rel = ii + (n_keys - w) - jj
            return jnp.where((rel >= 0) & (rel <= w),
                             -slope * (rel * dil).astype(jnp.float32), NEG_BIG)

        biases = {w: band_bias(w)}
        if n_per > 1:
            biases[2 * w] = band_bias(2 * w)
        n_blocks = seq // w

        def score_stage(b0, w=w, n_per=n_per, n_blocks=n_blocks, biases=biases,
                        q_ref=q_ref, k_ref=k_ref):
            blocks = []
            for n in range(b0, min(b0 + _B_BATCH, n_blocks)):
                res, nb = divmod(n, n_per)
                n_keys = w if nb == 0 else 2 * w
                key_rows = pl.ds((n + 1) * w - n_keys, n_keys)
                q = q_ref[0, pl.ds(n * w, w), :]
                s = lax.dot_general(k_ref[0, key_rows, :], q, (((1,), (1,)), ((), ())),
                                    preferred_element_type=jnp.float32) + biases[n_keys]
                blocks.append((res, nb, key_rows, s))
            return blocks

        def value_stage(blocks, w=w, dil=dil, g=g, v_ref=v_ref):
            soft = []
            for res, nb, key_rows, s in blocks:
                m = jnp.max(s, axis=0, keepdims=True)
                p = jnp.exp2(s - m)
                den = jnp.sum(p, axis=0, keepdims=True)
                soft.append((p.astype(v_ref.dtype), den, m + jnp.log2(den)))
            for (res, nb, key_rows, _), (p, den, lse) in zip(blocks, soft):
                o_t = jnp.dot(v_ref[0, :, key_rows], p, preferred_element_type=jnp.float32) / den
                t0 = nb * (w * dil) + res
                dst = pl.ds(t0, w, stride=dil) if dil > 1 else pl.ds(t0, w)
                out_sc[g, dst, :] = o_t.T
                lse_sc[g, dst, :] = jnp.broadcast_to(lse, (hd, w)).T

        stages += [(score_stage, b0, value_stage) for b0 in range(0, n_blocks, _B_BATCH)]

    for score_stage, b0, value_stage in stages:
        value_stage(score_stage(b0))

    tc = COMBINE_ROWS

    def combine(c, carry):
        rows = pl.ds(pl.multiple_of(c * tc, tc), tc)
        lses = [lse_sc[g, rows, :] for g in range(len(groups))]
        top = functools.reduce(jnp.maximum, lses)
        wts = [jnp.exp2(l - top) for l in lses]
        num = functools.reduce(lambda a, b: a + b,
                               [wt * out_sc[g, rows, :] for g, wt in enumerate(wts)])
        den = functools.reduce(lambda a, b: a + b, wts)
        z = z_ref[0, rows, :].astype(jnp.float32)
        o_ref[0, rows, :] = (num * z / (den * (1.0 + jnp.exp(-z)))).astype(o_ref.dtype)
        return carry

    lax.fori_loop(0, seq // tc, combine, 0)


def _attention_b(projs, vts, batch, seq):
    hd = B_HEAD_DIM
    nh = projs[1].shape[0] // 2
    assert all(seq % window == 0 for window, _ in B_GROUPS)
    slab = lambda off: pl.BlockSpec((1, seq, hd), lambda b, h: (off + h, b, 0))
    vt_spec = pl.BlockSpec((1, hd, seq), lambda b, h: (b, h, 0))
    return pl.pallas_call(
        functools.partial(_attn_b_kernel, seq=seq, n_heads=nh, groups=B_GROUPS),
        grid=(batch, nh),
        in_specs=[slab(0), slab(nh), vt_spec,
                  slab(0), slab(nh), vt_spec,
                  slab(0), slab(nh), vt_spec,
                  slab(2 * nh)],
        out_specs=pl.BlockSpec((1, seq, hd), lambda b, h: (h, b, 0)),
        out_shape=jax.ShapeDtypeStruct((nh, batch * seq, hd), jnp.bfloat16),
        scratch_shapes=[pltpu.VMEM((len(B_GROUPS), seq, hd), jnp.float32),
                        pltpu.VMEM((len(B_GROUPS), seq, hd), jnp.float32)],
        compiler_params=_params("parallel", "parallel"),
        name="dilated_attention",
    )(projs[0], projs[0], vts[0], projs[1], projs[1], vts[1], projs[2], projs[2], vts[2], projs[0])


def kernel(x, norm_a, w_in_a, lam_q1, lam_k1, lam_q2, lam_k2, subln_a, w_out_a,
           norm_b, w_in_b, w_out_b, norm_f):
    batch, seq, d = x.shape
    m = batch * seq
    bf16 = jnp.bfloat16
    x2 = x.reshape(m, d)
    tn = PROJ_TILE
    a_tile = ATTN_A_TILE

    lam_init = 0.8 - 0.6 * math.exp(-0.3 * 0)
    q_scale_a = A_HEAD_DIM ** -0.5 * LOG2E
    hn, vt_a = _norm_project_transposed(x2, norm_a[0], w_in_a[0], d, 2, "norm_in_proj_a_vt",
                                        tm=a_tile)
    v_blk0, v_blk1 = 2 * d // tn, 3 * d // tn
    col_map_a = lambda j: jnp.where(j < v_blk0, j, j + (v_blk1 - v_blk0))
    proj_a = _project(hn, w_in_a[0], 3 * d, col_map_a, 2 * A_HEAD_DIM, "in_proj_a",
                      n_scaled=d // tn, col_scale=q_scale_a, tn=tn)
    gated_a = _attention_a(proj_a, vt_a, lam_q1[0], lam_k1[0], lam_q2[0], lam_k2[0], subln_a[0],
                           batch, seq, lam_init, tile=a_tile)
    h1, hn1, hn4, hn16 = _out_project_a(gated_a, w_out_a[0].astype(bf16), x2, norm_b[0], batch, seq)

    q_scale_b = B_HEAD_DIM ** -0.5 * LOG2E
    w_in = w_in_b[0]
    blk = d // tn
    gate_blk0 = len(B_GROUPS) * 3 * blk
    vt_tn = VT_COL_TILE
    projs, vts = [], []
    for g, hn_g in enumerate((hn1, hn4.reshape(m, d), hn16.reshape(m, d))):
        q_blk0 = g * 3 * blk
        n_blk = 2 * blk + (blk if g == 0 else 0)
        col_map = lambda j, q_blk0=q_blk0: jnp.where(j < 2 * blk, q_blk0 + j, gate_blk0 + j - 2 * blk)
        projs.append(_project(hn_g, w_in, n_blk * tn, col_map, B_HEAD_DIM, f"in_proj_b{g}",
                              n_scaled=blk, col_scale=q_scale_b, tn=tn))
        vts.append(_project_transposed(hn_g, w_in, d, (q_blk0 + 2 * blk) * tn // vt_tn,
                                       f"in_proj_b{g}_vt", tm=seq, tn=vt_tn))
    gated_b = _attention_b(projs, vts, batch, seq)
    out = _out_project_b(gated_b, w_out_b[0].astype(bf16), h1, norm_f)
    return out.reshape(batch, seq, d)
```

```python
import functools
import math

import jax
import jax.numpy as jnp
from jax import lax
from jax.experimental import pallas as pl
from jax.experimental.pallas import tpu as pltpu

A_HEAD_DIM = 128
B_GROUPS = ((128, 1), (512, 4), (2048, 16))
B_HEAD_DIM = 128
RMS_EPS = 1e-6
NEG_BIG = -1e30
LOG2E = math.log2(math.e)
LANES = 128

VMEM_LIMIT_BYTES = 48 * 1024 * 1024
PROJ_VMEM_LIMIT_BYTES = 58 * 1024 * 1024

PROJ_TILE = 1024
PROJ_ROWS = 2048
VT_COL_TILE = 1024
ATTN_A_TILE = 512
OUT_A_ROWS = 512
OUT_B_ROWS = 1024
COMBINE_ROWS = 512


def _params(*semantics, vmem_limit_bytes=VMEM_LIMIT_BYTES):
    return pltpu.CompilerParams(dimension_semantics=semantics, vmem_limit_bytes=vmem_limit_bytes)


def _alibi_slope_log2(h, n_heads):
    return LOG2E * jnp.exp2(jnp.zeros((1, 1), jnp.float32)
                            - (8.0 / n_heads) * (h + 1).astype(jnp.float32))


def _silu(z):
    return z * jax.nn.sigmoid(z)


def _aligned(start, multiple):
    return start if isinstance(start, int) else pl.multiple_of(start, multiple)


def _proj_kernel(a_ref, w_ref, o_ref, w_sc, *, cb, n_scaled, col_scale):
    @pl.when(pl.program_id(1) == 0)
    def _():
        w_sc[...] = w_ref[...].astype(w_sc.dtype)

    acc = jnp.dot(a_ref[...], w_sc[...], preferred_element_type=jnp.float32)
    if n_scaled:
        acc = acc * jnp.where(pl.program_id(0) < n_scaled, col_scale, 1.0)
    for c in range(o_ref.shape[0]):
        o_ref[c] = acc[:, c * cb:(c + 1) * cb].astype(o_ref.dtype)


def _project(a, w, n_out, col_block_map, cb, name, n_scaled=0, col_scale=1.0,
             tm=PROJ_ROWS, tn=PROJ_TILE):
    m, k = a.shape
    return pl.pallas_call(
        functools.partial(_proj_kernel, cb=cb, n_scaled=n_scaled, col_scale=col_scale),
        grid=(n_out // tn, m // tm),
        in_specs=[pl.BlockSpec((tm, k), lambda j, i: (i, 0)),
                  pl.BlockSpec((k, tn), lambda j, i: (0, col_block_map(j)))],
        out_specs=pl.BlockSpec((tn // cb, tm, cb), lambda j, i: (j, i, 0)),
        out_shape=jax.ShapeDtypeStruct((n_out // cb, m, cb), jnp.bfloat16),
        scratch_shapes=[pltpu.VMEM((k, tn), jnp.bfloat16)],
        compiler_params=_params("arbitrary", "arbitrary", vmem_limit_bytes=PROJ_VMEM_LIMIT_BYTES),
        name=name,
    )(a, w)


def _norm_proj_t_kernel(x_ref, g_ref, w_ref, hn_ref, o_ref, wt_sc):
    @pl.when(pl.program_id(0) == 0)
    def _():
        wt_sc[...] = w_ref[...].T.astype(wt_sc.dtype)

    x = x_ref[...]
    ms = jnp.mean(x * x, axis=-1, keepdims=True)
    hn = (x * lax.rsqrt(ms + RMS_EPS) * g_ref[...]).astype(hn_ref.dtype)
    hn_ref[...] = hn
    o_ref[0] = lax.dot_general(wt_sc[...], hn, (((1,), (1,)), ((), ())),
                               preferred_element_type=jnp.float32).astype(o_ref.dtype)


def _norm_project_transposed(x, gain, w, n_out, col_block0, name, tm):
    m, k = x.shape
    const = lambda shape, index: pl.BlockSpec(shape, lambda i: index, pipeline_mode=pl.Buffered(1))
    return pl.pallas_call(
        _norm_proj_t_kernel,
        grid=(m // tm,),
        in_specs=[pl.BlockSpec((tm, k), lambda i: (i, 0)),
                  const((1, k), (0, 0)),
                  const((k, n_out), (0, col_block0))],
        out_specs=[pl.BlockSpec((tm, k), lambda i: (i, 0)),
                   pl.BlockSpec((1, n_out, tm), lambda i: (i, 0, 0))],
        out_shape=[jax.ShapeDtypeStruct((m, k), jnp.bfloat16),
                   jax.ShapeDtypeStruct((m // tm, n_out, tm), jnp.bfloat16)],
        scratch_shapes=[pltpu.VMEM((n_out, k), jnp.bfloat16)],
        compiler_params=_params("arbitrary"),
        name=name,
    )(x, gain.reshape(1, k), w)


def _proj_t_kernel(a_ref, w_ref, o_ref, wt_sc):
    @pl.when(pl.program_id(1) == 0)
    def _():
        wt_sc[...] = w_ref[...].T.astype(wt_sc.dtype)

    o_ref[0] = lax.dot_general(wt_sc[...], a_ref[...], (((1,), (1,)), ((), ())),
                               preferred_element_type=jnp.float32).astype(o_ref.dtype)


def _project_transposed(a, w, n_out, col_block0, name, tm, tn=512):
    m, k = a.shape
    return pl.pallas_call(
        _proj_t_kernel,
        grid=(n_out // tn, m // tm),
        in_specs=[pl.BlockSpec((tm, k), lambda j, i: (i, 0)),
                  pl.BlockSpec((k, tn), lambda j, i: (0, col_block0 + j))],
        out_specs=pl.BlockSpec((1, tn, tm), lambda j, i: (i, j, 0)),
        out_shape=jax.ShapeDtypeStruct((m // tm, n_out, tm), jnp.bfloat16),
        scratch_shapes=[pltpu.VMEM((tn, k), jnp.bfloat16)],
        compiler_params=_params("arbitrary", "arbitrary", vmem_limit_bytes=PROJ_VMEM_LIMIT_BYTES),
        name=name,
    )(a, w)


def _gather_slabs(a_ref):
    return jnp.concatenate([a_ref[c] for c in range(a_ref.shape[0])], axis=-1)


def _out_proj_a_kernel(a_ref, w_ref, r_ref, g_ref, h_ref, hn1_ref, hn4_ref, hn16_ref,
                       slab_sc, slab4_sc):
    y = jnp.dot(_gather_slabs(a_ref), w_ref[...], preferred_element_type=jnp.float32) + r_ref[...]
    h_ref[...] = y
    ms = jnp.mean(y * y, axis=-1, keepdims=True)
    hn = y * lax.rsqrt(ms + RMS_EPS) * g_ref[...]
    hn1_ref[...] = hn.astype(hn1_ref.dtype)
    tm, d = hn.shape
    n_slab = d // LANES
    for c in range(n_slab):
        slab_sc[c] = hn[:, c * LANES:(c + 1) * LANES]
    dil, dil2 = hn4_ref.shape[1], hn16_ref.shape[1]
    step2 = dil2 // dil
    assert dil * step2 == dil2
    n4 = tm // dil
    for r in range(dil):
        pieces = [slab_sc[c, pl.ds(r, n4, stride=dil), :] for c in range(n_slab)]
        hn4_ref[0, r] = jnp.concatenate(pieces, axis=-1).astype(hn4_ref.dtype)
        for c in range(n_slab):
            slab4_sc[c, r * n4:(r + 1) * n4, :] = pieces[c]
    for r in range(dil2):
        rows = jnp.concatenate(
            [slab4_sc[c, pl.ds((r % dil) * n4 + r // dil, tm // dil2, stride=step2), :]
             for c in range(n_slab)], axis=-1)
        hn16_ref[0, r] = rows.astype(hn16_ref.dtype)


def _out_project_a(a, w, resid, gain, batch, seq, tm=OUT_A_ROWS):
    nc, m, cb = a.shape
    k, n = w.shape
    nb = seq // tm
    const = lambda shape: pl.BlockSpec(shape, lambda i: (0,) * len(shape),
                                       pipeline_mode=pl.Buffered(1))
    perm_spec = lambda dil: pl.BlockSpec((1, dil, tm // dil, n), lambda i: (i // nb, 0, i % nb, 0))
    perm_shape = lambda dil: jax.ShapeDtypeStruct((batch, dil, seq // dil, n), jnp.bfloat16)
    return pl.pallas_call(
        _out_proj_a_kernel,
        grid=(m // tm,),
        in_specs=[pl.BlockSpec((nc, tm, cb), lambda i: (0, i, 0)),
                  const((k, n)),
                  pl.BlockSpec((tm, n), lambda i: (i, 0)),
                  const((1, n))],
        out_specs=[pl.BlockSpec((tm, n), lambda i: (i, 0)),
                   pl.BlockSpec((tm, n), lambda i: (i, 0)),
                   perm_spec(4), perm_spec(16)],
        out_shape=[jax.ShapeDtypeStruct((m, n), jnp.float32),
                   jax.ShapeDtypeStruct((m, n), jnp.bfloat16),
                   perm_shape(4), perm_shape(16)],
        scratch_shapes=[pltpu.VMEM((n // LANES, tm, LANES), jnp.float32)] * 2,
        compiler_params=_params("parallel", vmem_limit_bytes=PROJ_VMEM_LIMIT_BYTES),
        name="out_proj_a",
    )(a, w, resid, gain.reshape(1, n))


def _out_proj_b_kernel(a_ref, w_ref, r_ref, g_ref, o_ref):
    y = jnp.dot(_gather_slabs(a_ref), w_ref[...], preferred_element_type=jnp.float32) + r_ref[...]
    ms = jnp.mean(y * y, axis=-1, keepdims=True)
    o_ref[...] = y * lax.rsqrt(ms + RMS_EPS) * g_ref[...]


def _out_project_b(a, w, resid, gain, tm=OUT_B_ROWS):
    nc, m, cb = a.shape
    k, n = w.shape
    const = lambda shape: pl.BlockSpec(shape, lambda i: (0,) * len(shape),
                                       pipeline_mode=pl.Buffered(1))
    return pl.pallas_call(
        _out_proj_b_kernel,
        grid=(m // tm,),
        in_specs=[pl.BlockSpec((nc, tm, cb), lambda i: (0, i, 0)),
                  const((k, n)),
                  pl.BlockSpec((tm, n), lambda i: (i, 0)),
                  const((1, n))],
        out_specs=pl.BlockSpec((tm, n), lambda i: (i, 0)),
        out_shape=jax.ShapeDtypeStruct((m, n), jnp.float32),
        compiler_params=_params("parallel", vmem_limit_bytes=PROJ_VMEM_LIMIT_BYTES),
        name="out_proj_b",
    )(a, w, resid, gain.reshape(1, n))


_SLOPE_PIECES = 3
_POS_RADIX = 64

def _attn_a_kernel(q_ref, k_ref, vt_ref, z_ref, lq1_ref, lk1_ref, lq2_ref, lk2_ref, sub_ref,
                   o_ref, s_sc, kfeat_sc, mask_sc, *, tile, n_heads, lam_init):
    hd = A_HEAD_DIM
    seq = q_ref.shape[1]
    nt = seq // tile
    h = pl.program_id(1)
    slope = _alibi_slope_log2(h, n_heads)
    lam = (jnp.exp(jnp.sum(lq1_ref[...] * lk1_ref[...], axis=-1, keepdims=True))
           - jnp.exp(jnp.sum(lq2_ref[...] * lk2_ref[...], axis=-1, keepdims=True)) + lam_init)

    bf16 = k_ref.dtype
    col = lax.broadcasted_iota(jnp.int32, (tile, hd), 1)
    row = lax.broadcasted_iota(jnp.int32, (tile, hd), 0)
    pieces, rest = [], slope
    for _ in range(_SLOPE_PIECES):
        piece = rest.astype(bf16).astype(jnp.float32)
        pieces.append(piece)
        rest = rest - piece
    q_feat = jnp.zeros((tile, hd), jnp.float32)
    for n, piece in enumerate(pieces):
        q_feat = jnp.where(col == n, _POS_RADIX * piece, q_feat)
        q_feat = jnp.where(col == _SLOPE_PIECES + n, piece, q_feat)
    q_feat = q_feat.astype(bf16)
    for kt in range(nt):
        pos = row + kt * tile
        hi = (pos // _POS_RADIX).astype(jnp.float32)
        lo = (pos % _POS_RADIX).astype(jnp.float32)
        kfeat_sc[kt] = jnp.where(col < _SLOPE_PIECES, hi,
                                 jnp.where(col < 2 * _SLOPE_PIECES, lo, 0.0)).astype(bf16)
    half = tile // 2
    mask_sc[...] = jnp.where(lax.broadcasted_iota(jnp.int32, (half, half), 1)
                             >= lax.broadcasted_iota(jnp.int32, (half, half), 0), 0.0, -jnp.inf)

    def panels(qt):
        k0 = qt * tile
        return ([((kt * tile, (kt + 1) * tile), (0, tile), None) for kt in range(qt)]
                + [((k0, k0 + half), (0, half), 0), ((k0, k0 + tile), (half, tile), half)])

    def scores(qt):
        q = q_ref[0, pl.ds(qt * tile, tile), :]
        for i in range(2):
            q_aug = jnp.concatenate([q[:, i * hd:(i + 1) * hd], q_feat], axis=1)
            maxima = [None, None]
            for (k0, k1), (c0, c1), mask_row in panels(qt):
                k_aug = jnp.concatenate([k_ref[0, k0:k1, i * hd:(i + 1) * hd],
                                         kfeat_sc[k0 // tile, k0 % tile:k0 % tile + k1 - k0, :]], axis=1)
                s = lax.dot_general(k_aug, q_aug[c0:c1], (((1,), (1,)), ((), ())),
                                    preferred_element_type=jnp.float32)
                if mask_row == 0:
                    s = s + mask_sc[...]
                elif mask_row is not None:
                    s = jnp.concatenate([s[:mask_row], s[mask_row:] + mask_sc[...]], axis=0)
                s_sc[qt % 2, i, k0:k1, c0:c1] = s
                top = jnp.max(s, axis=0, keepdims=True)
                for hf in range(2):
                    lo, hi = max(c0, hf * half), min(c1, (hf + 1) * half)
                    if lo < hi:
                        piece = top[:, lo - c0:hi - c0]
                        maxima[hf] = piece if maxima[hf] is None else jnp.maximum(maxima[hf], piece)
            col_max[qt, i] = jnp.concatenate(maxima, axis=1)

    col_max = {}
    sub_gain = sub_ref[...] * (1.0 - lam_init)

    def finish(qt):
        heads = []
        for i in range(2):
            top = col_max.pop((qt, i))
            sums = [None, None]
            out = [None, None]
            for (k0, k1), (c0, c1), _ in panels(qt):
                p = jnp.exp2(s_sc[qt % 2, i, k0:k1, c0:c1] - top[:, c0:c1])
                part = jnp.sum(p, axis=0, keepdims=True)
                vtc = vt_ref[k0 // tile][:, k0 % tile:k0 % tile + k1 - k0]
                pv = jnp.dot(vtc, p.astype(vtc.dtype), preferred_element_type=jnp.float32)
                for hf in range(2):
                    lo, hi = max(c0, hf * half), min(c1, (hf + 1) * half)
                    if lo < hi:
                        add = lambda old, new: new if old is None else old + new
                        sums[hf] = add(sums[hf], part[:, lo - c0:hi - c0])
                        out[hf] = add(out[hf], pv[:, lo - c0:hi - c0])
            heads.append(jnp.concatenate(out, axis=1) / jnp.concatenate(sums, axis=1))
        rows = pl.ds(qt * tile, tile)
        o = (heads[0] - lam * heads[1]).T
        ms = jnp.mean(o * o, axis=-1, keepdims=True)
        o = o * lax.rsqrt(ms + RMS_EPS) * sub_gain
        o_ref[0, rows, :] = (o * _silu(z_ref[0, rows, :].astype(jnp.float32))).astype(o_ref.dtype)

    scores(0)
    for qt in range(nt):
        if qt + 1 < nt:
            scores(qt + 1)
        finish(qt)


def _attention_a(proj, vt, lam_q1, lam_k1, lam_q2, lam_k2, subln, batch, seq, lam_init, tile):
    w2 = 2 * A_HEAD_DIM
    nh = proj.shape[0] // 3
    nt = seq // tile
    assert vt.shape[2] == tile
    vec = lambda a: a.reshape(1, -1)
    vec_spec = pl.BlockSpec((1, A_HEAD_DIM), lambda b, h: (0, 0))
    slab = lambda off: pl.BlockSpec((1, seq, w2), lambda b, h: (off + h, b, 0))
    return pl.pallas_call(
        functools.partial(_attn_a_kernel, tile=tile, n_heads=nh, lam_init=lam_init),
        grid=(batch, nh),
        in_specs=[slab(0), slab(nh),
                  pl.BlockSpec((nt, w2, tile), lambda b, h: (b, h, 0)),
                  slab(2 * nh),
                  vec_spec, vec_spec, vec_spec, vec_spec,
                  pl.BlockSpec((1, w2), lambda b, h: (0, 0))],
        out_specs=slab(0),
        out_shape=jax.ShapeDtypeStruct((nh, batch * seq, w2), jnp.bfloat16),
        scratch_shapes=[pltpu.VMEM((2, 2, seq, tile), jnp.float32),
                        pltpu.VMEM((nt, tile, A_HEAD_DIM), jnp.bfloat16),
                        pltpu.VMEM((tile // 2, tile // 2), jnp.float32)],
        compiler_params=_params("parallel", "arbitrary"),
        name="diff_attention",
    )(proj, proj, vt, proj, vec(lam_q1), vec(lam_k1), vec(lam_q2), vec(lam_k2), vec(subln))


_B_BATCH = 8


def _attn_b_kernel(q0_ref, k0_ref, v0_ref, q1_ref, k1_ref, v1_ref, q2_ref, k2_ref, v2_ref, z_ref,
                   o_ref, out_sc, lse_sc, *, seq, n_heads, groups):
    hd = B_HEAD_DIM
    h = pl.program_id(1)
    slope = _alibi_slope_log2(h, n_heads)
    qkv = ((q0_ref, k0_ref, v0_ref), (q1_ref, k1_ref, v1_ref), (q2_ref, k2_ref, v2_ref))

    stages = []
    for g, (window, dil) in enumerate(groups):
        w = window // dil
        n_per = (seq // dil) // w
        q_ref, k_ref, v_ref = qkv[g]

        def band_bias(n_keys, w=w, dil=dil):
            jj = lax.broadcasted_iota(jnp.int32, (n_keys, w), 0)
            ii = lax.broadcasted_iota(jnp.int32, (n_keys, w), 1)
            rel = ii + (n_keys - w) - jj
            return jnp.where((rel >= 0) & (rel <= w),
                             -slope * (rel * dil).astype(jnp.float32), NEG_BIG)

        biases = {w: band_bias(w)}
        if n_per > 1:
            biases[2 * w] = band_bias(2 * w)
        n_blocks = seq // w

        def score_stage(b0, w=w, n_per=n_per, n_blocks=n_blocks, biases=biases,
                        q_ref=q_ref, k_ref=k_ref):
            blocks = []
            for n in range(b0, min(b0 + _B_BATCH, n_blocks)):
                res, nb = divmod(n, n_per)
                n_keys = w if nb == 0 else 2 * w
                key_rows = pl.ds((n + 1) * w - n_keys, n_keys)
                q = q_ref[0, pl.ds(n * w, w), :]
                s = lax.dot_general(k_ref[0, key_rows, :], q, (((1,), (1,)), ((), ())),
                                    preferred_element_type=jnp.float32) + biases[n_keys]
                blocks.append((res, nb, key_rows, s))
            return blocks

        def value_stage(blocks, w=w, dil=dil, g=g, v_ref=v_ref):
            soft = []
            for res, nb, key_rows, s in blocks:
                m = jnp.max(s, axis=0, keepdims=True)
                p = jnp.exp2(s - m)
                den = jnp.sum(p, axis=0, keepdims=True)
                soft.append((p.astype(v_ref.dtype), den, m + jnp.log2(den)))
            for (res, nb, key_rows, _), (p, den, lse) in zip(blocks, soft):
                o_t = jnp.dot(v_ref[0, :, key_rows], p, preferred_element_type=jnp.float32) / den
                t0 = nb * (w * dil) + res
                dst = pl.ds(t0, w, stride=dil) if dil > 1 else pl.ds(t0, w)
                out_sc[g, dst, :] = o_t.T
                lse_sc[g, dst, :] = jnp.broadcast_to(lse, (hd, w)).T

        stages += [(score_stage, b0, value_stage) for b0 in range(0, n_blocks, _B_BATCH)]

    for score_stage, b0, value_stage in stages:
        value_stage(score_stage(b0))

    tc = COMBINE_ROWS

    def combine(c, carry):
        rows = pl.ds(pl.multiple_of(c * tc, tc), tc)
        lses = [lse_sc[g, rows, :] for g in range(len(groups))]
        top = functools.reduce(jnp.maximum, lses)
        wts = [jnp.exp2(l - top) for l in lses]
        num = functools.reduce(lambda a, b: a + b,
                               [wt * out_sc[g, rows, :] for g, wt in enumerate(wts)])
        den = functools.reduce(lambda a, b: a + b, wts)
        z = z_ref[0, rows, :].astype(jnp.float32)
        o_ref[0, rows, :] = (num * z / (den * (1.0 + jnp.exp(-z)))).astype(o_ref.dtype)
        return carry

    lax.fori_loop(0, seq // tc, combine, 0)


def _attention_b(projs, vts, batch, seq):
    hd = B_HEAD_DIM
    nh = projs[1].shape[0] // 2
    assert all(seq % window == 0 for window, _ in B_GROUPS)
    slab = lambda off: pl.BlockSpec((1, seq, hd), lambda b, h: (off + h, b, 0))
    vt_spec = pl.BlockSpec((1, hd, seq), lambda b, h: (b, h, 0))
    return pl.pallas_call(
        functools.partial(_attn_b_kernel, seq=seq, n_heads=nh, groups=B_GROUPS),
        grid=(batch, nh),
        in_specs=[slab(0), slab(nh), vt_spec,
                  slab(0), slab(nh), vt_spec,
                  slab(0), slab(nh), vt_spec,
                  slab(2 * nh)],
        out_specs=pl.BlockSpec((1, seq, hd), lambda b, h: (h, b, 0)),
        out_shape=jax.ShapeDtypeStruct((nh, batch * seq, hd), jnp.bfloat16),
        scratch_shapes=[pltpu.VMEM((len(B_GROUPS), seq, hd), jnp.float32),
                        pltpu.VMEM((len(B_GROUPS), seq, hd), jnp.float32)],
        compiler_params=_params("parallel", "parallel"),
        name="dilated_attention",
    )(projs[0], projs[0], vts[0], projs[1], projs[1], vts[1], projs[2], projs[2], vts[2], projs[0])


def kernel(x, norm_a, w_in_a, lam_q1, lam_k1, lam_q2, lam_k2, subln_a, w_out_a,
           norm_b, w_in_b, w_out_b, norm_f):
    batch, seq, d = x.shape
    m = batch * seq
    bf16 = jnp.bfloat16
    x2 = x.reshape(m, d)
    tn = PROJ_TILE
    a_tile = ATTN_A_TILE

    lam_init = 0.8 - 0.6 * math.exp(-0.3 * 0)
    q_scale_a = A_HEAD_DIM ** -0.5 * LOG2E
    hn, vt_a = _norm_project_transposed(x2, norm_a[0], w_in_a[0], d, 2, "norm_in_proj_a_vt",
                                        tm=a_tile)
    v_blk0, v_blk1 = 2 * d // tn, 3 * d // tn
    col_map_a = lambda j: jnp.where(j < v_blk0, j, j + (v_blk1 - v_blk0))
    proj_a = _project(hn, w_in_a[0], 3 * d, col_map_a, 2 * A_HEAD_DIM, "in_proj_a",
                      n_scaled=d // tn, col_scale=q_scale_a, tn=tn)
    gated_a = _attention_a(proj_a, vt_a, lam_q1[0], lam_k1[0], lam_q2[0], lam_k2[0], subln_a[0],
                           batch, seq, lam_init, tile=a_tile)
    h1, hn1, hn4, hn16 = _out_project_a(gated_a, w_out_a[0].astype(bf16), x2, norm_b[0], batch, seq)

    q_scale_b = B_HEAD_DIM ** -0.5 * LOG2E
    w_in = w_in_b[0]
    blk = d // tn
    gate_blk0 = len(B_GROUPS) * 3 * blk
    vt_tn = VT_COL_TILE
    projs, vts = [], []
    for g, hn_g in enumerate((hn1, hn4.reshape(m, d), hn16.reshape(m, d))):
        q_blk0 = g * 3 * blk
        n_blk = 2 * blk + (blk if g == 0 else 0)
        col_map = lambda j, q_blk0=q_blk0: jnp.where(j < 2 * blk, q_blk0 + j, gate_blk0 + j - 2 * blk)
        projs.append(_project(hn_g, w_in, n_blk * tn, col_map, B_HEAD_DIM, f"in_proj_b{g}",
                              n_scaled=blk, col_scale=q_scale_b, tn=tn))
        vts.append(_project_transposed(hn_g, w_in, d, (q_blk0 + 2 * blk) * tn // vt_tn,
                                       f"in_proj_b{g}_vt", tm=seq, tn=vt_tn))
    gated_b = _attention_b(projs, vts, batch, seq)
    out = _out_project_b(gated_b, w_out_b[0].astype(bf16), h1, norm_f)
    return out.reshape(batch, seq, d)
```

```python
import functools
import math

import jax
import jax.numpy as jnp
from jax import lax
from jax.experimental import pallas as pl
from jax.experimental.pallas import tpu as pltpu

A_HEAD_DIM = 128
B_GROUPS = ((128, 1), (512, 4), (2048, 16))
B_HEAD_DIM = 128
RMS_EPS = 1e-6
NEG_BIG = -1e30
LOG2E = math.log2(math.e)
LANES = 128

VMEM_LIMIT_BYTES = 48 * 1024 * 1024
PROJ_VMEM_LIMIT_BYTES = 58 * 1024 * 1024

PROJ_TILE = 1024
PROJ_ROWS = 2048
VT_COL_TILE = 1024
ATTN_A_TILE = 512
OUT_A_ROWS = 512
OUT_B_ROWS = 1024
COMBINE_ROWS = 512


def _params(*semantics, vmem_limit_bytes=VMEM_LIMIT_BYTES):
    return pltpu.CompilerParams(dimension_semantics=semantics, vmem_limit_bytes=vmem_limit_bytes)


def _alibi_slope_log2(h, n_heads):
    return LOG2E * jnp.exp2(jnp.zeros((1, 1), jnp.float32)
                            - (8.0 / n_heads) * (h + 1).astype(jnp.float32))


def _silu(z):
    return z * jax.nn.sigmoid(z)


def _aligned(start, multiple):
    return start if isinstance(start, int) else pl.multiple_of(start, multiple)


def _proj_kernel(a_ref, w_ref, o_ref, w_sc, *, cb, n_scaled, col_scale):
    @pl.when(pl.program_id(1) == 0)
    def _():
        w_sc[...] = w_ref[...].astype(w_sc.dtype)

    acc = jnp.dot(a_ref[...], w_sc[...], preferred_element_type=jnp.float32)
    if n_scaled:
        acc = acc * jnp.where(pl.program_id(0) < n_scaled, col_scale, 1.0)
    for c in range(o_ref.shape[0]):
        o_ref[c] = acc[:, c * cb:(c + 1) * cb].astype(o_ref.dtype)


def _project(a, w, n_out, col_block_map, cb, name, n_scaled=0, col_scale=1.0,
             tm=PROJ_ROWS, tn=PROJ_TILE):
    m, k = a.shape
    return pl.pallas_call(
        functools.partial(_proj_kernel, cb=cb, n_scaled=n_scaled, col_scale=col_scale),
        grid=(n_out // tn, m // tm),
        in_specs=[pl.BlockSpec((tm, k), lambda j, i: (i, 0)),
                  pl.BlockSpec((k, tn), lambda j, i: (0, col_block_map(j)))],
        out_specs=pl.BlockSpec((tn // cb, tm, cb), lambda j, i: (j, i, 0)),
        out_shape=jax.ShapeDtypeStruct((n_out // cb, m, cb), jnp.bfloat16),
        scratch_shapes=[pltpu.VMEM((k, tn), jnp.bfloat16)],
        compiler_params=_params("arbitrary", "arbitrary", vmem_limit_bytes=PROJ_VMEM_LIMIT_BYTES),
        name=name,
    )(a, w)


def _norm_proj_t_kernel(x_ref, g_ref, w_ref, hn_ref, o_ref, wt_sc):
    @pl.when(pl.program_id(0) == 0)
    def _():
        wt_sc[...] = w_ref[...].T.astype(wt_sc.dtype)

    x = x_ref[...]
    ms = jnp.mean(x * x, axis=-1, keepdims=True)
    hn = (x * lax.rsqrt(ms + RMS_EPS) * g_ref[...]).astype(hn_ref.dtype)
    hn_ref[...] = hn
    o_ref[0] = lax.dot_general(wt_sc[...], hn, (((1,), (1,)), ((), ())),
                               preferred_element_type=jnp.float32).astype(o_ref.dtype)


def _norm_project_transposed(x, gain, w, n_out, col_block0, name, tm):
    m, k = x.shape
    const = lambda shape, index: pl.BlockSpec(shape, lambda i: index, pipeline_mode=pl.Buffered(1))
    return pl.pallas_call(
        _norm_proj_t_kernel,
        grid=(m // tm,),
        in_specs=[pl.BlockSpec((tm, k), lambda i: (i, 0)),
                  const((1, k), (0, 0)),
                  const((k, n_out), (0, col_block0))],
        out_specs=[pl.BlockSpec((tm, k), lambda i: (i, 0)),
                   pl.BlockSpec((1, n_out, tm), lambda i: (i, 0, 0))],
        out_shape=[jax.ShapeDtypeStruct((m, k), jnp.bfloat16),
                   jax.ShapeDtypeStruct((m // tm, n_out, tm), jnp.bfloat16)],
        scratch_shapes=[pltpu.VMEM((n_out, k), jnp.bfloat16)],
        compiler_params=_params("arbitrary"),
        name=name,
    )(x, gain.reshape(1, k), w)


def _proj_t_kernel(a_ref, w_ref, o_ref, wt_sc):
    @pl.when(pl.program_id(1) == 0)
    def _():
        wt_sc[...] = w_ref[...].T.astype(wt_sc.dtype)

    o_ref[0] = lax.dot_general(wt_sc[...], a_ref[...], (((1,), (1,)), ((), ())),
                               preferred_element_type=jnp.float32).astype(o_ref.dtype)


def _project_transposed(a, w, n_out, col_block0, name, tm, tn=512):
    m, k = a.shape
    return pl.pallas_call(
        _proj_t_kernel,
        grid=(n_out // tn, m // tm),
        in_specs=[pl.BlockSpec((tm, k), lambda j, i: (i, 0)),
                  pl.BlockSpec((k, tn), lambda j, i: (0, col_block0 + j))],
        out_specs=pl.BlockSpec((1, tn, tm), lambda j, i: (i, j, 0)),
        out_shape=jax.ShapeDtypeStruct((m // tm, n_out, tm), jnp.bfloat16),
        scratch_shapes=[pltpu.VMEM((tn, k), jnp.bfloat16)],
        compiler_params=_params("arbitrary", "arbitrary", vmem_limit_bytes=PROJ_VMEM_LIMIT_BYTES),
        name=name,
    )(a, w)


def _gather_slabs(a_ref):
    return jnp.concatenate([a_ref[c] for c in range(a_ref.shape[0])], axis=-1)


def _out_proj_a_kernel(a_ref, w_ref, r_ref, g_ref, h_ref, hn1_ref, hn4_ref, hn16_ref,
                       slab_sc, slab4_sc):
    y = jnp.dot(_gather_slabs(a_ref), w_ref[...], preferred_element_type=jnp.float32) + r_ref[...]
    h_ref[...] = y
    ms = jnp.mean(y * y, axis=-1, keepdims=True)
    hn = y * lax.rsqrt(ms + RMS_EPS) * g_ref[...]
    hn1_ref[...] = hn.astype(hn1_ref.dtype)
    tm, d = hn.shape
    n_slab = d // LANES
    for c in range(n_slab):
        slab_sc[c] = hn[:, c * LANES:(c + 1) * LANES]
    dil, dil2 = hn4_ref.shape[1], hn16_ref.shape[1]
    step2 = dil2 // dil
    assert dil * step2 == dil2
    n4 = tm // dil
    for r in range(dil):
        pieces = [slab_sc[c, pl.ds(r, n4, stride=dil), :] for c in range(n_slab)]
        hn4_ref[0, r] = jnp.concatenate(pieces, axis=-1).astype(hn4_ref.dtype)
        for c in range(n_slab):
            slab4_sc[c, r * n4:(r + 1) * n4, :] = pieces[c]
    for r in range(dil2):
        rows = jnp.concatenate(
            [slab4_sc[c, pl.ds((r % dil) * n4 + r // dil, tm // dil2, stride=step2), :]
             for c in range(n_slab)], axis=-1)
        hn16_ref[0, r] = rows.astype(hn16_ref.dtype)


def _out_project_a(a, w, resid, gain, batch, seq, tm=OUT_A_ROWS):
    nc, m, cb = a.shape
    k, n = w.shape
    nb = seq // tm
    const = lambda shape: pl.BlockSpec(shape, lambda i: (0,) * len(shape),
                                       pipeline_mode=pl.Buffered(1))
    perm_spec = lambda dil: pl.BlockSpec((1, dil, tm // dil, n), lambda i: (i // nb, 0, i % nb, 0))
    perm_shape = lambda dil: jax.ShapeDtypeStruct((batch, dil, seq // dil, n), jnp.bfloat16)
    return pl.pallas_call(
        _out_proj_a_kernel,
        grid=(m // tm,),
        in_specs=[pl.BlockSpec((nc, tm, cb), lambda i: (0, i, 0)),
                  const((k, n)),
                  pl.BlockSpec((tm, n), lambda i: (i, 0)),
                  const((1, n))],
        out_specs=[pl.BlockSpec((tm, n), lambda i: (i, 0)),
                   pl.BlockSpec((tm, n), lambda i: (i, 0)),
                   perm_spec(4), perm_spec(16)],
        out_shape=[jax.ShapeDtypeStruct((m, n), jnp.float32),
                   jax.ShapeDtypeStruct((m, n), jnp.bfloat16),
                   perm_shape(4), perm_shape(16)],
        scratch_shapes=[pltpu.VMEM((n // LANES, tm, LANES), jnp.float32)] * 2,
        compiler_params=_params("parallel", vmem_limit_bytes=PROJ_VMEM_LIMIT_BYTES),
        name="out_proj_a",
    )(a, w, resid, gain.reshape(1, n))


def _out_proj_b_kernel(a_ref, w_ref, r_ref, g_ref, o_ref):
    y = jnp.dot(_gather_slabs(a_ref), w_ref[...], preferred_element_type=jnp.float32) + r_ref[...]
    ms = jnp.mean(y * y, axis=-1, keepdims=True)
    o_ref[...] = y * lax.rsqrt(ms + RMS_EPS) * g_ref[...]


def _out_project_b(a, w, resid, gain, tm=OUT_B_ROWS):
    nc, m, cb = a.shape
    k, n = w.shape
    const = lambda shape: pl.BlockSpec(shape, lambda i: (0,) * len(shape),
                                       pipeline_mode=pl.Buffered(1))
    return pl.pallas_call(
        _out_proj_b_kernel,
        grid=(m // tm,),
        in_specs=[pl.BlockSpec((nc, tm, cb), lambda i: (0, i, 0)),
                  const((k, n)),
                  pl.BlockSpec((tm, n), lambda i: (i, 0)),
                  const((1, n))],
        out_specs=pl.BlockSpec((tm, n), lambda i: (i, 0)),
        out_shape=jax.ShapeDtypeStruct((m, n), jnp.float32),
        compiler_params=_params("parallel", vmem_limit_bytes=PROJ_VMEM_LIMIT_BYTES),
        name="out_proj_b",
    )(a, w, resid, gain.reshape(1, n))


_SLOPE_PIECES = 3
_POS_RADIX = 64

_A_HEADS_PER_STEP = 2


def _attn_a_kernel(*refs, **static):
    for hh in range(_A_HEADS_PER_STEP):
        _attn_a_head(hh, *refs, **static)


def _attn_a_head(hh, q_ref, k_ref, vt_ref, z_ref, lq1_ref, lk1_ref, lq2_ref, lk2_ref, sub_ref,
                 o_ref, s_sc, kfeat_sc, mask_sc, *, tile, n_heads, lam_init):
    hd = A_HEAD_DIM
    seq = q_ref.shape[1]
    nt = seq // tile
    w2 = 2 * hd
    h = _A_HEADS_PER_STEP * pl.program_id(1) + hh
    slope = _alibi_slope_log2(h, n_heads)
    lam = (jnp.exp(jnp.sum(lq1_ref[...] * lk1_ref[...], axis=-1, keepdims=True))
           - jnp.exp(jnp.sum(lq2_ref[...] * lk2_ref[...], axis=-1, keepdims=True)) + lam_init)

    bf16 = k_ref.dtype
    col = lax.broadcasted_iota(jnp.int32, (tile, hd), 1)
    row = lax.broadcasted_iota(jnp.int32, (tile, hd), 0)
    pieces, rest = [], slope
    for _ in range(_SLOPE_PIECES):
        piece = rest.astype(bf16).astype(jnp.float32)
        pieces.append(piece)
        rest = rest - piece
    q_feat = jnp.zeros((tile, hd), jnp.float32)
    for n, piece in enumerate(pieces):
        q_feat = jnp.where(col == n, _POS_RADIX * piece, q_feat)
        q_feat = jnp.where(col == _SLOPE_PIECES + n, piece, q_feat)
    q_feat = q_feat.astype(bf16)
    for kt in range(nt):
        pos = row + kt * tile
        hi = (pos // _POS_RADIX).astype(jnp.float32)
        lo = (pos % _POS_RADIX).astype(jnp.float32)
        kfeat_sc[kt] = jnp.where(col < _SLOPE_PIECES, hi,
                                 jnp.where(col < 2 * _SLOPE_PIECES, lo, 0.0)).astype(bf16)
    half = tile // 2
    mask_sc[...] = jnp.where(lax.broadcasted_iota(jnp.int32, (half, half), 1)
                             >= lax.broadcasted_iota(jnp.int32, (half, half), 0), 0.0, -jnp.inf)

    def panels(qt):
        k0 = qt * tile
        return ([((kt * tile, (kt + 1) * tile), (0, tile), None) for kt in range(qt)]
                + [((k0, k0 + half), (0, half), 0), ((k0, k0 + tile), (half, tile), half)])

    def scores(qt):
        q = q_ref[hh, pl.ds(qt * tile, tile), :]
        for i in range(2):
            q_aug = jnp.concatenate([q[:, i * hd:(i + 1) * hd], q_feat], axis=1)
            maxima = [None, None]
            for (k0, k1), (c0, c1), mask_row in panels(qt):
                k_aug = jnp.concatenate([k_ref[hh, k0:k1, i * hd:(i + 1) * hd],
                                         kfeat_sc[k0 // tile, k0 % tile:k0 % tile + k1 - k0, :]], axis=1)
                s = lax.dot_general(k_aug, q_aug[c0:c1], (((1,), (1,)), ((), ())),
                                    preferred_element_type=jnp.float32)
                if mask_row == 0:
                    s = s + mask_sc[...]
                elif mask_row is not None:
                    s = jnp.concatenate([s[:mask_row], s[mask_row:] + mask_sc[...]], axis=0)
                s_sc[qt % 2, i, k0:k1, c0:c1] = s
                top = jnp.max(s, axis=0, keepdims=True)
                for hf in range(2):
                    lo, hi = max(c0, hf * half), min(c1, (hf + 1) * half)
                    if lo < hi:
                        piece = top[:, lo - c0:hi - c0]
                        maxima[hf] = piece if maxima[hf] is None else jnp.maximum(maxima[hf], piece)
            col_max[qt, i] = jnp.concatenate(maxima, axis=1)

    col_max = {}
    sub_gain = sub_ref[...] * (1.0 - lam_init)

    def finish(qt):
        heads = []
        for i in range(2):
            top = col_max.pop((qt, i))
            sums = [None, None]
            out = [None, None]
            for (k0, k1), (c0, c1), _ in panels(qt):
                p = jnp.exp2(s_sc[qt % 2, i, k0:k1, c0:c1] - top[:, c0:c1])
                part = jnp.sum(p, axis=0, keepdims=True)
                vtc = vt_ref[k0 // tile, hh * w2:(hh + 1) * w2,
                             k0 % tile:k0 % tile + k1 - k0]
                pv = jnp.dot(vtc, p.astype(vtc.dtype), preferred_element_type=jnp.float32)
                for hf in range(2):
                    lo, hi = max(c0, hf * half), min(c1, (hf + 1) * half)
                    if lo < hi:
                        add = lambda old, new: new if old is None else old + new
                        sums[hf] = add(sums[hf], part[:, lo - c0:hi - c0])
                        out[hf] = add(out[hf], pv[:, lo - c0:hi - c0])
            heads.append(jnp.concatenate(out, axis=1) / jnp.concatenate(sums, axis=1))
        rows = pl.ds(qt * tile, tile)
        o = (heads[0] - lam * heads[1]).T
        ms = jnp.mean(o * o, axis=-1, keepdims=True)
        o = o * lax.rsqrt(ms + RMS_EPS) * sub_gain
        o_ref[hh, rows, :] = (o * _silu(z_ref[hh, rows, :].astype(jnp.float32))).astype(o_ref.dtype)

    scores(0)
    for qt in range(nt):
        if qt + 1 < nt:
            scores(qt + 1)
        finish(qt)


def _attention_a(proj, vt, lam_q1, lam_k1, lam_q2, lam_k2, subln, batch, seq, lam_init, tile):
    w2 = 2 * A_HEAD_DIM
    nh = proj.shape[0] // 3
    nt = seq // tile
    assert vt.shape[2] == tile
    vec = lambda a: a.reshape(1, -1)
    vec_spec = pl.BlockSpec((1, A_HEAD_DIM), lambda b, h: (0, 0))
    hps = _A_HEADS_PER_STEP
    assert nh % hps == 0
    slab = lambda off: pl.BlockSpec((hps, seq, w2), lambda b, h: (off // hps + h, b, 0))
    return pl.pallas_call(
        functools.partial(_attn_a_kernel, tile=tile, n_heads=nh, lam_init=lam_init),
        grid=(batch, nh // hps),
        in_specs=[slab(0), slab(nh),
                  pl.BlockSpec((nt, hps * w2, tile), lambda b, h: (b, h, 0)),
                  slab(2 * nh),
                  vec_spec, vec_spec, vec_spec, vec_spec,
                  pl.BlockSpec((1, w2), lambda b, h: (0, 0))],
        out_specs=slab(0),
        out_shape=jax.ShapeDtypeStruct((nh, batch * seq, w2), jnp.bfloat16),
        scratch_shapes=[pltpu.VMEM((2, 2, seq, tile), jnp.float32),
                        pltpu.VMEM((nt, tile, A_HEAD_DIM), jnp.bfloat16),
                        pltpu.VMEM((tile // 2, tile // 2), jnp.float32)],
        compiler_params=_params("parallel", "arbitrary"),
        name="diff_attention",
    )(proj, proj, vt, proj, vec(lam_q1), vec(lam_k1), vec(lam_q2), vec(lam_k2), vec(subln))


_B_BATCH = 8


_B_HEADS_PER_STEP = 2


def _attn_b_kernel(*refs, **static):
    for hh in range(_B_HEADS_PER_STEP):
        _attn_b_head(hh, *refs, **static)


def _attn_b_head(hh, q0_ref, k0_ref, v0_ref, q1_ref, k1_ref, v1_ref, q2_ref, k2_ref, v2_ref, z_ref,
                 o_ref, out_sc, lse_sc, *, seq, n_heads, groups):
    hd = B_HEAD_DIM
    h = _B_HEADS_PER_STEP * pl.program_id(1) + hh
    slope = _alibi_slope_log2(h, n_heads)
    qkv = ((q0_ref, k0_ref, v0_ref), (q1_ref, k1_ref, v1_ref), (q2_ref, k2_ref, v2_ref))

    stages = []
    for g, (window, dil) in enumerate(groups):
        w = window // dil
        n_per = (seq // dil) // w
        q_ref, k_ref, v_ref = qkv[g]

        def band_bias(n_keys, w=w, dil=dil):
            jj = lax.broadcasted_iota(jnp.int32, (n_keys, w), 0)
            ii = lax.broadcasted_iota(jnp.int32, (n_keys, w), 1)
            rel = ii + (n_keys - w) - jj
            return jnp.where((rel >= 0) & (rel <= w),
                             -slope * (rel * dil).astype(jnp.float32), NEG_BIG)

        biases = {w: band_bias(w)}
        if n_per > 1:
            biases[2 * w] = band_bias(2 * w)
        n_blocks = seq // w

        def score_stage(b0, w=w, n_per=n_per, n_blocks=n_blocks, biases=biases,
                        q_ref=q_ref, k_ref=k_ref):
            blocks = []
            for n in range(b0, min(b0 + _B_BATCH, n_blocks)):
                res, nb = divmod(n, n_per)
                n_keys = w if nb == 0 else 2 * w
                key_rows = pl.ds((n + 1) * w - n_keys, n_keys)
                q = q_ref[hh, pl.ds(n * w, w), :]
                s = lax.dot_general(k_ref[hh, key_rows, :], q, (((1,), (1,)), ((), ())),
                                    preferred_element_type=jnp.float32) + biases[n_keys]
                blocks.append((res, nb, key_rows, s))
            return blocks

        def value_stage(blocks, w=w, dil=dil, g=g, v_ref=v_ref):
            soft = []
            for res, nb, key_rows, s in blocks:
                m = jnp.max(s, axis=0, keepdims=True)
                p = jnp.exp2(s - m)
                den = jnp.sum(p, axis=0, keepdims=True)
                soft.append((p.astype(v_ref.dtype), den, m + jnp.log2(den)))
            for (res, nb, key_rows, _), (p, den, lse) in zip(blocks, soft):
                o_t = jnp.dot(v_ref[0, hh * hd:(hh + 1) * hd, key_rows], p,
                              preferred_element_type=jnp.float32) / den
                t0 = nb * (w * dil) + res
                dst = pl.ds(t0, w, stride=dil) if dil > 1 else pl.ds(t0, w)
                out_sc[g, dst, :] = o_t.T
                lse_sc[g, dst, :] = jnp.broadcast_to(lse, (hd, w)).T

        stages += [(score_stage, b0, value_stage) for b0 in range(0, n_blocks, _B_BATCH)]

    for score_stage, b0, value_stage in stages:
        value_stage(score_stage(b0))

    tc = COMBINE_ROWS

    def combine(c, carry):
        rows = pl.ds(pl.multiple_of(c * tc, tc), tc)
        lses = [lse_sc[g, rows, :] for g in range(len(groups))]
        top = functools.reduce(jnp.maximum, lses)
        wts = [jnp.exp2(l - top) for l in lses]
        num = functools.reduce(lambda a, b: a + b,
                               [wt * out_sc[g, rows, :] for g, wt in enumerate(wts)])
        den = functools.reduce(lambda a, b: a + b, wts)
        z = z_ref[hh, rows, :].astype(jnp.float32)
        o_ref[hh, rows, :] = (num * z / (den * (1.0 + jnp.exp(-z)))).astype(o_ref.dtype)
        return carry

    lax.fori_loop(0, seq // tc, combine, 0)


def _attention_b(projs, vts, batch, seq):
    hd = B_HEAD_DIM
    nh = projs[1].shape[0] // 2
    assert all(seq % window == 0 for window, _ in B_GROUPS)
    hps = _B_HEADS_PER_STEP
    assert nh % hps == 0
    slab = lambda off: pl.BlockSpec((hps, seq, hd), lambda b, h: (off // hps + h, b, 0))
    vt_spec = pl.BlockSpec((1, hps * hd, seq), lambda b, h: (b, h, 0))
    return pl.pallas_call(
        functools.partial(_attn_b_kernel, seq=seq, n_heads=nh, groups=B_GROUPS),
        grid=(batch, nh // hps),
        in_specs=[slab(0), slab(nh), vt_spec,
                  slab(0), slab(nh), vt_spec,
                  slab(0), slab(nh), vt_spec,
                  slab(2 * nh)],
        out_specs=pl.BlockSpec((hps, seq, hd), lambda b, h: (h, b, 0)),
        out_shape=jax.ShapeDtypeStruct((nh, batch * seq, hd), jnp.bfloat16),
        scratch_shapes=[pltpu.VMEM((len(B_GROUPS), seq, hd), jnp.float32),
                        pltpu.VMEM((len(B_GROUPS), seq, hd), jnp.float32)],
        compiler_params=_params("parallel", "parallel"),
        name="dilated_attention",
    )(projs[0], projs[0], vts[0], projs[1], projs[1], vts[1], projs[2], projs[2], vts[2], projs[0])


def kernel(x, norm_a, w_in_a, lam_q1, lam_k1, lam_q2, lam_k2, subln_a, w_out_a,
           norm_b, w_in_b, w_out_b, norm_f):
    batch, seq, d = x.shape
    m = batch * seq
    bf16 = jnp.bfloat16
    x2 = x.reshape(m, d)
    tn = PROJ_TILE
    a_tile = ATTN_A_TILE

    lam_init = 0.8 - 0.6 * math.exp(-0.3 * 0)
    q_scale_a = A_HEAD_DIM ** -0.5 * LOG2E
    hn, vt_a = _norm_project_transposed(x2, norm_a[0], w_in_a[0], d, 2, "norm_in_proj_a_vt",
                                        tm=a_tile)
    v_blk0, v_blk1 = 2 * d // tn, 3 * d // tn
    col_map_a = lambda j: jnp.where(j < v_blk0, j, j + (v_blk1 - v_blk0))
    proj_a = _project(hn, w_in_a[0], 3 * d, col_map_a, 2 * A_HEAD_DIM, "in_proj_a",
                      n_scaled=d // tn, col_scale=q_scale_a, tn=tn)
    gated_a = _attention_a(proj_a, vt_a, lam_q1[0], lam_k1[0], lam_q2[0], lam_k2[0], subln_a[0],
                           batch, seq, lam_init, tile=a_tile)
    h1, hn1, hn4, hn16 = _out_project_a(gated_a, w_out_a[0].astype(bf16), x2, norm_b[0], batch, seq)

    q_scale_b = B_HEAD_DIM ** -0.5 * LOG2E
    w_in = w_in_b[0]
    blk = d // tn
    gate_blk0 = len(B_GROUPS) * 3 * blk
    vt_tn = VT_COL_TILE
    projs, vts = [], []
    for g, hn_g in enumerate((hn1, hn4.reshape(m, d), hn16.reshape(m, d))):
        q_blk0 = g * 3 * blk
        n_blk = 2 * blk + (blk if g == 0 else 0)
        col_map = lambda j, q_blk0=q_blk0: jnp.where(j < 2 * blk, q_blk0 + j, gate_blk0 + j - 2 * blk)
        projs.append(_project(hn_g, w_in, n_blk * tn, col_map, B_HEAD_DIM, f"in_proj_b{g}",
                              n_scaled=blk, col_scale=q_scale_b, tn=tn))
        vts.append(_project_transposed(hn_g, w_in, d, (q_blk0 + 2 * blk) * tn // vt_tn,
                                       f"in_proj_b{g}_vt", tm=seq, tn=vt_tn))
    gated_b = _attention_b(projs, vts, batch, seq)
    out = _out_project_b(gated_b, w_out_b[0].astype(bf16), h1, norm_f)
    return out.reshape(batch, seq, d)
```

```python
import functools
import math

import jax
import jax.numpy as jnp
from jax import lax
from jax.experimental import pallas as pl
from jax.experimental.pallas import tpu as pltpu

A_HEAD_DIM = 128
B_GROUPS = ((128, 1), (512, 4), (2048, 16))
B_HEAD_DIM = 128
RMS_EPS = 1e-6
NEG_BIG = -1e30
LOG2E = math.log2(math.e)
LANES = 128

VMEM_LIMIT_BYTES = 48 * 1024 * 1024
PROJ_VMEM_LIMIT_BYTES = 58 * 1024 * 1024

PROJ_TILE = 1024
PROJ_ROWS = 2048
VT_COL_TILE = 1024
ATTN_A_TILE = 512
OUT_A_ROWS = 512
OUT_B_ROWS = 1024
COMBINE_ROWS = 512


def _params(*semantics, vmem_limit_bytes=VMEM_LIMIT_BYTES):
    return pltpu.CompilerParams(dimension_semantics=semantics, vmem_limit_bytes=vmem_limit_bytes)


def _alibi_slope_log2(h, n_heads):
    return LOG2E * jnp.exp2(jnp.zeros((1, 1), jnp.float32)
                            - (8.0 / n_heads) * (h + 1).astype(jnp.float32))


def _silu(z):
    return z * jax.nn.sigmoid(z)


def _aligned(start, multiple):
    return start if isinstance(start, int) else pl.multiple_of(start, multiple)


def _proj_kernel(a_ref, w_ref, o_ref, w_sc, *, cb, n_scaled, col_scale):
    @pl.when(pl.program_id(1) == 0)
    def _():
        w_sc[...] = w_ref[...].astype(w_sc.dtype)

    acc = jnp.dot(a_ref[...], w_sc[...], preferred_element_type=jnp.float32)
    if n_scaled:
        acc = acc * jnp.where(pl.program_id(0) < n_scaled, col_scale, 1.0)
    for c in range(o_ref.shape[0]):
        o_ref[c] = acc[:, c * cb:(c + 1) * cb].astype(o_ref.dtype)


def _project(a, w, n_out, col_block_map, cb, name, n_scaled=0, col_scale=1.0,
             tm=PROJ_ROWS, tn=PROJ_TILE):
    m, k = a.shape
    return pl.pallas_call(
        functools.partial(_proj_kernel, cb=cb, n_scaled=n_scaled, col_scale=col_scale),
        grid=(n_out // tn, m // tm),
        in_specs=[pl.BlockSpec((tm, k), lambda j, i: (i, 0)),
                  pl.BlockSpec((k, tn), lambda j, i: (0, col_block_map(j)))],
        out_specs=pl.BlockSpec((tn // cb, tm, cb), lambda j, i: (j, i, 0)),
        out_shape=jax.ShapeDtypeStruct((n_out // cb, m, cb), jnp.bfloat16),
        scratch_shapes=[pltpu.VMEM((k, tn), jnp.bfloat16)],
        compiler_params=_params("arbitrary", "arbitrary", vmem_limit_bytes=PROJ_VMEM_LIMIT_BYTES),
        name=name,
    )(a, w)


def _norm_proj_t_kernel(x_ref, g_ref, w_ref, hn_ref, o_ref, wt_sc):
    @pl.when(pl.program_id(0) == 0)
    def _():
        wt_sc[...] = w_ref[...].T.astype(wt_sc.dtype)

    x = x_ref[...]
    ms = jnp.mean(x * x, axis=-1, keepdims=True)
    hn = (x * lax.rsqrt(ms + RMS_EPS) * g_ref[...]).astype(hn_ref.dtype)
    hn_ref[...] = hn
    o_ref[0] = lax.dot_general(wt_sc[...], hn, (((1,), (1,)), ((), ())),
                               preferred_element_type=jnp.float32).astype(o_ref.dtype)


def _norm_project_transposed(x, gain, w, n_out, col_block0, name, tm):
    m, k = x.shape
    const = lambda shape, index: pl.BlockSpec(shape, lambda i: index, pipeline_mode=pl.Buffered(1))
    return pl.pallas_call(
        _norm_proj_t_kernel,
        grid=(m // tm,),
        in_specs=[pl.BlockSpec((tm, k), lambda i: (i, 0)),
                  const((1, k), (0, 0)),
                  const((k, n_out), (0, col_block0))],
        out_specs=[pl.BlockSpec((tm, k), lambda i: (i, 0)),
                   pl.BlockSpec((1, n_out, tm), lambda i: (i, 0, 0))],
        out_shape=[jax.ShapeDtypeStruct((m, k), jnp.bfloat16),
                   jax.ShapeDtypeStruct((m // tm, n_out, tm), jnp.bfloat16)],
        scratch_shapes=[pltpu.VMEM((n_out, k), jnp.bfloat16)],
        compiler_params=_params("arbitrary"),
        name=name,
    )(x, gain.reshape(1, k), w)


def _proj_t_kernel(a_ref, w_ref, o_ref, wt_sc):
    @pl.when(pl.program_id(1) == 0)
    def _():
        wt_sc[...] = w_ref[...].T.astype(wt_sc.dtype)

    o_ref[0] = lax.dot_general(wt_sc[...], a_ref[...], (((1,), (1,)), ((), ())),
                               preferred_element_type=jnp.float32).astype(o_ref.dtype)


def _project_transposed(a, w, n_out, col_block0, name, tm, tn=512):
    m, k = a.shape
    return pl.pallas_call(
        _proj_t_kernel,
        grid=(n_out // tn, m // tm),
        in_specs=[pl.BlockSpec((tm, k), lambda j, i: (i, 0)),
                  pl.BlockSpec((k, tn), lambda j, i: (0, col_block0 + j))],
        out_specs=pl.BlockSpec((1, tn, tm), lambda j, i: (i, j, 0)),
        out_shape=jax.ShapeDtypeStruct((m // tm, n_out, tm), jnp.bfloat16),
        scratch_shapes=[pltpu.VMEM((tn, k), jnp.bfloat16)],
        compiler_params=_params("arbitrary", "arbitrary", vmem_limit_bytes=PROJ_VMEM_LIMIT_BYTES),
        name=name,
    )(a, w)


def _gather_slabs(a_ref):
    return jnp.concatenate([a_ref[c] for c in range(a_ref.shape[0])], axis=-1)


def _out_proj_a_kernel(a_ref, w_ref, r_ref, g_ref, h_ref, hn1_ref, hn4_ref, hn16_ref,
                       slab_sc, slab4_sc):
    y = jnp.dot(_gather_slabs(a_ref), w_ref[...], preferred_element_type=jnp.float32) + r_ref[...]
    h_ref[...] = y
    ms = jnp.mean(y * y, axis=-1, keepdims=True)
    hn = y * lax.rsqrt(ms + RMS_EPS) * g_ref[...]
    hn1_ref[...] = hn.astype(hn1_ref.dtype)
    tm, d = hn.shape
    n_slab = d // LANES
    for c in range(n_slab):
        slab_sc[c] = hn[:, c * LANES:(c + 1) * LANES]
    dil, dil2 = hn4_ref.shape[1], hn16_ref.shape[1]
    step2 = dil2 // dil
    assert dil * step2 == dil2
    n4 = tm // dil
    for r in range(dil):
        pieces = [slab_sc[c, pl.ds(r, n4, stride=dil), :] for c in range(n_slab)]
        hn4_ref[0, r] = jnp.concatenate(pieces, axis=-1).astype(hn4_ref.dtype)
        for c in range(n_slab):
            slab4_sc[c, r * n4:(r + 1) * n4, :] = pieces[c]
    for r in range(dil2):
        rows = jnp.concatenate(
            [slab4_sc[c, pl.ds((r % dil) * n4 + r // dil, tm // dil2, stride=step2), :]
             for c in range(n_slab)], axis=-1)
        hn16_ref[0, r] = rows.astype(hn16_ref.dtype)


def _out_project_a(a, w, resid, gain, batch, seq, tm=OUT_A_ROWS):
    nc, m, cb = a.shape
    k, n = w.shape
    nb = seq // tm
    const = lambda shape: pl.BlockSpec(shape, lambda i: (0,) * len(shape),
                                       pipeline_mode=pl.Buffered(1))
    perm_spec = lambda dil: pl.BlockSpec((1, dil, tm // dil, n), lambda i: (i // nb, 0, i % nb, 0))
    perm_shape = lambda dil: jax.ShapeDtypeStruct((batch, dil, seq // dil, n), jnp.bfloat16)
    return pl.pallas_call(
        _out_proj_a_kernel,
        grid=(m // tm,),
        in_specs=[pl.BlockSpec((nc, tm, cb), lambda i: (0, i, 0)),
                  const((k, n)),
                  pl.BlockSpec((tm, n), lambda i: (i, 0)),
                  const((1, n))],
        out_specs=[pl.BlockSpec((tm, n), lambda i: (i, 0)),
                   pl.BlockSpec((tm, n), lambda i: (i, 0)),
                   perm_spec(4), perm_spec(16)],
        out_shape=[jax.ShapeDtypeStruct((m, n), jnp.float32),
                   jax.ShapeDtypeStruct((m, n), jnp.bfloat16),
                   perm_shape(4), perm_shape(16)],
        scratch_shapes=[pltpu.VMEM((n // LANES, tm, LANES), jnp.float32)] * 2,
        compiler_params=_params("parallel", vmem_limit_bytes=PROJ_VMEM_LIMIT_BYTES),
        name="out_proj_a",
    )(a, w, resid, gain.reshape(1, n))


def _out_proj_b_kernel(a_ref, w_ref, r_ref, g_ref, o_ref):
    y = jnp.dot(_gather_slabs(a_ref), w_ref[...], preferred_element_type=jnp.float32) + r_ref[...]
    ms = jnp.mean(y * y, axis=-1, keepdims=True)
    o_ref[...] = y * lax.rsqrt(ms + RMS_EPS) * g_ref[...]


def _out_project_b(a, w, resid, gain, tm=OUT_B_ROWS):
    nc, m, cb = a.shape
    k, n = w.shape
    const = lambda shape: pl.BlockSpec(shape, lambda i: (0,) * len(shape),
                                       pipeline_mode=pl.Buffered(1))
    return pl.pallas_call(
        _out_proj_b_kernel,
        grid=(m // tm,),
        in_specs=[pl.BlockSpec((nc, tm, cb), lambda i: (0, i, 0)),
                  const((k, n)),
                  pl.BlockSpec((tm, n), lambda i: (i, 0)),
                  const((1, n))],
        out_specs=pl.BlockSpec((tm, n), lambda i: (i, 0)),
        out_shape=jax.ShapeDtypeStruct((m, n), jnp.float32),
        compiler_params=_params("parallel", vmem_limit_bytes=PROJ_VMEM_LIMIT_BYTES),
        name="out_proj_b",
    )(a, w, resid, gain.reshape(1, n))


_SLOPE_PIECES = 3
_POS_RADIX = 64

_A_HEADS_PER_STEP = 2


def _attn_a_kernel(*refs, **static):
    for hh in range(_A_HEADS_PER_STEP):
        _attn_a_head(hh, *refs, **static)


def _attn_a_head(hh, q_ref, k_ref, vt_ref, z_ref, lq1_ref, lk1_ref, lq2_ref, lk2_ref, sub_ref,
                 o_ref, s_sc, kfeat_sc, mask_sc, *, tile, n_heads, lam_init):
    hd = A_HEAD_DIM
    seq = q_ref.shape[1]
    nt = seq // tile
    w2 = 2 * hd
    h = _A_HEADS_PER_STEP * pl.program_id(1) + hh
    slope = _alibi_slope_log2(h, n_heads)
    lam = (jnp.exp(jnp.sum(lq1_ref[...] * lk1_ref[...], axis=-1, keepdims=True))
           - jnp.exp(jnp.sum(lq2_ref[...] * lk2_ref[...], axis=-1, keepdims=True)) + lam_init)

    bf16 = k_ref.dtype
    col = lax.broadcasted_iota(jnp.int32, (tile, hd), 1)
    row = lax.broadcasted_iota(jnp.int32, (tile, hd), 0)
    pieces, rest = [], slope
    for _ in range(_SLOPE_PIECES):
        piece = rest.astype(bf16).astype(jnp.float32)
        pieces.append(piece)
        rest = rest - piece
    q_feat = jnp.zeros((tile, hd), jnp.float32)
    for n, piece in enumerate(pieces):
        q_feat = jnp.where(col == n, _POS_RADIX * piece, q_feat)
        q_feat = jnp.where(col == _SLOPE_PIECES + n, piece, q_feat)
    q_feat = q_feat.astype(bf16)
    for kt in range(nt):
        pos = row + kt * tile
        hi = (pos // _POS_RADIX).astype(jnp.float32)
        lo = (pos % _POS_RADIX).astype(jnp.float32)
        kfeat_sc[kt] = jnp.where(col < _SLOPE_PIECES, hi,
                                 jnp.where(col < 2 * _SLOPE_PIECES, lo, 0.0)).astype(bf16)
    half = tile // 2
    mask_sc[...] = jnp.where(lax.broadcasted_iota(jnp.int32, (half, half), 1)
                             >= lax.broadcasted_iota(jnp.int32, (half, half), 0), 0.0, -jnp.inf)

    def panels(qt):
        k0 = qt * tile
        return ([((kt * tile, (kt + 1) * tile), (0, tile), None) for kt in range(qt)]
                + [((k0, k0 + half), (0, half), 0), ((k0, k0 + tile), (half, tile), half)])

    def scores(qt):
        q = q_ref[hh, pl.ds(qt * tile, tile), :]
        for i in range(2):
            q_aug = jnp.concatenate([q[:, i * hd:(i + 1) * hd], q_feat], axis=1)
            maxima = [None, None]
            for (k0, k1), (c0, c1), mask_row in panels(qt):
                k_aug = jnp.concatenate([k_ref[hh, k0:k1, i * hd:(i + 1) * hd],
                                         kfeat_sc[k0 // tile, k0 % tile:k0 % tile + k1 - k0, :]], axis=1)
                s = lax.dot_general(k_aug, q_aug[c0:c1], (((1,), (1,)), ((), ())),
                                    preferred_element_type=jnp.float32)
                if mask_row == 0:
                    s = s + mask_sc[...]
                elif mask_row is not None:
                    s = jnp.concatenate([s[:mask_row], s[mask_row:] + mask_sc[...]], axis=0)
                s_sc[0, i, k0:k1, c0:c1] = s
                top = jnp.max(s, axis=0, keepdims=True)
                for hf in range(2):
                    lo, hi = max(c0, hf * half), min(c1, (hf + 1) * half)
                    if lo < hi:
                        piece = top[:, lo - c0:hi - c0]
                        maxima[hf] = piece if maxima[hf] is None else jnp.maximum(maxima[hf], piece)
            col_max[qt, i] = jnp.concatenate(maxima, axis=1)

    col_max = {}
    sub_gain = sub_ref[...] * (1.0 - lam_init)

    def finish(qt):
        heads = []
        for i in range(2):
            top = col_max.pop((qt, i))
            sums = [None, None]
            out = [None, None]
            for (k0, k1), (c0, c1), _ in panels(qt):
                p = jnp.exp2(s_sc[0, i, k0:k1, c0:c1] - top[:, c0:c1])
                part = jnp.sum(p, axis=0, keepdims=True)
                vtc = vt_ref[k0 // tile, hh * w2:(hh + 1) * w2,
                             k0 % tile:k0 % tile + k1 - k0]
                pv = jnp.dot(vtc, p.astype(vtc.dtype), preferred_element_type=jnp.float32)
                for hf in range(2):
                    lo, hi = max(c0, hf * half), min(c1, (hf + 1) * half)
                    if lo < hi:
                        add = lambda old, new: new if old is None else old + new
                        sums[hf] = add(sums[hf], part[:, lo - c0:hi - c0])
                        out[hf] = add(out[hf], pv[:, lo - c0:hi - c0])
            heads.append(jnp.concatenate(out, axis=1) / jnp.concatenate(sums, axis=1))
        rows = pl.ds(qt * tile, tile)
        o = (heads[0] - lam * heads[1]).T
        ms = jnp.mean(o * o, axis=-1, keepdims=True)
        o = o * lax.rsqrt(ms + RMS_EPS) * sub_gain
        o_ref[hh, rows, :] = (o * _silu(z_ref[hh, rows, :].astype(jnp.float32))).astype(o_ref.dtype)

    for qt in range(nt):
        scores(qt)
        finish(qt)


def _attention_a(proj, vt, lam_q1, lam_k1, lam_q2, lam_k2, subln, batch, seq, lam_init, tile):
    w2 = 2 * A_HEAD_DIM
    nh = proj.shape[0] // 3
    nt = seq // tile
    assert vt.shape[2] == tile
    vec = lambda a: a.reshape(1, -1)
    vec_spec = pl.BlockSpec((1, A_HEAD_DIM), lambda b, h: (0, 0))
    hps = _A_HEADS_PER_STEP
    assert nh % hps == 0
    slab = lambda off: pl.BlockSpec((hps, seq, w2), lambda b, h: (off // hps + h, b, 0))
    return pl.pallas_call(
        functools.partial(_attn_a_kernel, tile=tile, n_heads=nh, lam_init=lam_init),
        grid=(batch, nh // hps),
        in_specs=[slab(0), slab(nh),
                  pl.BlockSpec((nt, hps * w2, tile), lambda b, h: (b, h, 0)),
                  slab(2 * nh),
                  vec_spec, vec_spec, vec_spec, vec_spec,
                  pl.BlockSpec((1, w2), lambda b, h: (0, 0))],
        out_specs=slab(0),
        out_shape=jax.ShapeDtypeStruct((nh, batch * seq, w2), jnp.bfloat16),
        scratch_shapes=[pltpu.VMEM((1, 2, seq, tile), jnp.float32),
                        pltpu.VMEM((nt, tile, A_HEAD_DIM), jnp.bfloat16),
                        pltpu.VMEM((tile // 2, tile // 2), jnp.float32)],
        compiler_params=_params("parallel", "arbitrary"),
        name="diff_attention",
    )(proj, proj, vt, proj, vec(lam_q1), vec(lam_k1), vec(lam_q2), vec(lam_k2), vec(subln))


_B_BATCH = 8


_B_HEADS_PER_STEP = 2


def _attn_b_kernel(*refs, **static):
    for hh in range(_B_HEADS_PER_STEP):
        _attn_b_head(hh, *refs, **static)


def _attn_b_head(hh, q0_ref, k0_ref, v0_ref, q1_ref, k1_ref, v1_ref, q2_ref, k2_ref, v2_ref, z_ref,
                 o_ref, out_sc, lse_sc, *, seq, n_heads, groups):
    hd = B_HEAD_DIM
    h = _B_HEADS_PER_STEP * pl.program_id(1) + hh
    slope = _alibi_slope_log2(h, n_heads)
    qkv = ((q0_ref, k0_ref, v0_ref), (q1_ref, k1_ref, v1_ref), (q2_ref, k2_ref, v2_ref))

    stages = []
    for g, (window, dil) in enumerate(groups):
        w = window // dil
        n_per = (seq // dil) // w
        q_ref, k_ref, v_ref = qkv[g]

        def band_bias(n_keys, w=w, dil=dil):
            jj = lax.broadcasted_iota(jnp.int32, (n_keys, w), 0)
            ii = lax.broadcasted_iota(jnp.int32, (n_keys, w), 1)
            rel = ii + (n_keys - w) - jj
            return jnp.where((rel >= 0) & (rel <= w),
                             -slope * (rel * dil).astype(jnp.float32), NEG_BIG)

        biases = {w: band_bias(w)}
        if n_per > 1:
            biases[2 * w] = band_bias(2 * w)
        n_blocks = seq // w

        def score_stage(b0, w=w, n_per=n_per, n_blocks=n_blocks, biases=biases,
                        q_ref=q_ref, k_ref=k_ref):
            blocks = []
            for n in range(b0, min(b0 + _B_BATCH, n_blocks)):
                res, nb = divmod(n, n_per)
                n_keys = w if nb == 0 else 2 * w
                key_rows = pl.ds((n + 1) * w - n_keys, n_keys)
                q = q_ref[hh, pl.ds(n * w, w), :]
                s = lax.dot_general(k_ref[hh, key_rows, :], q, (((1,), (1,)), ((), ())),
                                    preferred_element_type=jnp.float32) + biases[n_keys]
                blocks.append((res, nb, key_rows, s))
            return blocks

        def value_stage(blocks, w=w, dil=dil, g=g, v_ref=v_ref):
            soft = []
            for res, nb, key_rows, s in blocks:
                m = jnp.max(s, axis=0, keepdims=True)
                p = jnp.exp2(s - m)
                den = jnp.sum(p, axis=0, keepdims=True)
                soft.append((p.astype(v_ref.dtype), den, m + jnp.log2(den)))
            for (res, nb, key_rows, _), (p, den, lse) in zip(blocks, soft):
                o_t = jnp.dot(v_ref[0, hh * hd:(hh + 1) * hd, key_rows], p,
                              preferred_element_type=jnp.float32) / den
                t0 = nb * (w * dil) + res
                dst = pl.ds(t0, w, stride=dil) if dil > 1 else pl.ds(t0, w)
                out_sc[g, dst, :] = o_t.T
                lse_sc[g, dst, :] = jnp.broadcast_to(lse, (hd, w)).T

        stages += [(score_stage, b0, value_stage) for b0 in range(0, n_blocks, _B_BATCH)]

    for score_stage, b0, value_stage in stages:
        value_stage(score_stage(b0))

    tc = COMBINE_ROWS

    def combine(c, carry):
        rows = pl.ds(pl.multiple_of(c * tc, tc), tc)
        lses = [lse_sc[g, rows, :] for g in range(len(groups))]
        top = functools.reduce(jnp.maximum, lses)
        wts = [jnp.exp2(l - top) for l in lses]
        num = functools.reduce(lambda a, b: a + b,
                               [wt * out_sc[g, rows, :] for g, wt in enumerate(wts)])
        den = functools.reduce(lambda a, b: a + b, wts)
        z = z_ref[hh, rows, :].astype(jnp.float32)
        o_ref[hh, rows, :] = (num * z / (den * (1.0 + jnp.exp(-z)))).astype(o_ref.dtype)
        return carry

    lax.fori_loop(0, seq // tc, combine, 0)


def _attention_b(projs, vts, batch, seq):
    hd = B_HEAD_DIM
    nh = projs[1].shape[0] // 2
    assert all(seq % window == 0 for window, _ in B_GROUPS)
    hps = _B_HEADS_PER_STEP
    assert nh % hps == 0
    slab = lambda off: pl.BlockSpec((hps, seq, hd), lambda b, h: (off // hps + h, b, 0))
    vt_spec = pl.BlockSpec((1, hps * hd, seq), lambda b, h: (b, h, 0))
    return pl.pallas_call(
        functools.partial(_attn_b_kernel, seq=seq, n_heads=nh, groups=B_GROUPS),
        grid=(batch, nh // hps),
        in_specs=[slab(0), slab(nh), vt_spec,
                  slab(0), slab(nh), vt_spec,
                  slab(0), slab(nh), vt_spec,
                  slab(2 * nh)],
        out_specs=pl.BlockSpec((hps, seq, hd), lambda b, h: (h, b, 0)),
        out_shape=jax.ShapeDtypeStruct((nh, batch * seq, hd), jnp.bfloat16),
        scratch_shapes=[pltpu.VMEM((len(B_GROUPS), seq, hd), jnp.float32),
                        pltpu.VMEM((len(B_GROUPS), seq, hd), jnp.float32)],
        compiler_params=_params("parallel", "parallel"),
        name="dilated_attention",
    )(projs[0], projs[0], vts[0], projs[1], projs[1], vts[1], projs[2], projs[2], vts[2], projs[0])


def kernel(x, norm_a, w_in_a, lam_q1, lam_k1, lam_q2, lam_k2, subln_a, w_out_a,
           norm_b, w_in_b, w_out_b, norm_f):
    batch, seq, d = x.shape
    m = batch * seq
    bf16 = jnp.bfloat16
    x2 = x.reshape(m, d)
    tn = PROJ_TILE
    a_tile = ATTN_A_TILE

    lam_init = 0.8 - 0.6 * math.exp(-0.3 * 0)
    q_scale_a = A_HEAD_DIM ** -0.5 * LOG2E
    hn, vt_a = _norm_project_transposed(x2, norm_a[0], w_in_a[0], d, 2, "norm_in_proj_a_vt",
                                        tm=a_tile)
    v_blk0, v_blk1 = 2 * d // tn, 3 * d // tn
    col_map_a = lambda j: jnp.where(j < v_blk0, j, j + (v_blk1 - v_blk0))
    proj_a = _project(hn, w_in_a[0], 3 * d, col_map_a, 2 * A_HEAD_DIM, "in_proj_a",
                      n_scaled=d // tn, col_scale=q_scale_a, tn=tn)
    gated_a = _attention_a(proj_a, vt_a, lam_q1[0], lam_k1[0], lam_q2[0], lam_k2[0], subln_a[0],
                           batch, seq, lam_init, tile=a_tile)
    h1, hn1, hn4, hn16 = _out_project_a(gated_a, w_out_a[0].astype(bf16), x2, norm_b[0], batch, seq)

    q_scale_b = B_HEAD_DIM ** -0.5 * LOG2E
    w_in = w_in_b[0]
    blk = d // tn
    gate_blk0 = len(B_GROUPS) * 3 * blk
    vt_tn = VT_COL_TILE
    projs, vts = [], []
    for g, hn_g in enumerate((hn1, hn4.reshape(m, d), hn16.reshape(m, d))):
        q_blk0 = g * 3 * blk
        n_blk = 2 * blk + (blk if g == 0 else 0)
        col_map = lambda j, q_blk0=q_blk0: jnp.where(j < 2 * blk, q_blk0 + j, gate_blk0 + j - 2 * blk)
        projs.append(_project(hn_g, w_in, n_blk * tn, col_map, B_HEAD_DIM, f"in_proj_b{g}",
                              n_scaled=blk, col_scale=q_scale_b, tn=tn))
        vts.append(_project_transposed(hn_g, w_in, d, (q_blk0 + 2 * blk) * tn // vt_tn,
                                       f"in_proj_b{g}_vt", tm=seq, tn=vt_tn))
    gated_b = _attention_b(projs, vts, batch, seq)
    out = _out_project_b(gated_b, w_out_b[0].astype(bf16), h1, norm_f)
    return out.reshape(batch, seq, d)
```
